```python
import math
import jax, jax.numpy as jnp
from jax import lax
import numpy as np

D_MODEL = 1024
BATCH = 1
SEQ = 16384
DEPTH = 2
DEC_BATCH = 2
DEC_SEQ = 16384
PAST_LEN = 128

N_HEADS = 8
HEAD_DK = 128
HEAD_DV = 128
QK_W = N_HEADS * HEAD_DK
V_W = N_HEADS * HEAD_DV
CONV_K = 5
CHUNK = 64
N_DIR = 2
F_GROUPS = 4
F_GROUP_W = 128
F_W = F_GROUPS * F_GROUP_W
IN_W = 2 * QK_W + 2 * V_W + 4 * N_HEADS + 2 * F_W + 2 * D_MODEL
EPS = 1e-6

kernel_name = "hybrid_deltanet_fnet_encoder"


def _rmsnorm(x, w):
    xf = x.astype(jnp.float32)
    y = xf * lax.rsqrt(jnp.mean(xf * xf, axis=-1, keepdims=True) + EPS)
    return (y * w.astype(jnp.float32)).astype(x.dtype)


def _l2norm(x):
    return x * lax.rsqrt(jnp.sum(x * x, axis=-1, keepdims=True) + EPS)


def _centred_dwconv(x, w):
    c = x.shape[-1]
    pad = CONV_K // 2
    return lax.conv_general_dilated(
        x, w[:, None, :].astype(x.dtype), window_strides=(1,), padding=((pad, pad),),
        dimension_numbers=("NWC", "WIO", "NWC"), feature_group_count=c)


def _gated_delta_chunked(q, k, v, g, beta):
    B, T, H, DK = q.shape
    DV = v.shape[-1]
    N = T // CHUNK

    def to_chunks(t):
        t = jnp.moveaxis(t, 2, 1)
        return t.reshape((B, H, N, CHUNK) + t.shape[3:])

    q, k, v, g, beta = (to_chunks(t) for t in (q, k, v, g, beta))
    g = jnp.cumsum(g, axis=-1)
    tril = jnp.tril(jnp.ones((CHUNK, CHUNK), dtype=bool))
    strict = jnp.tril(jnp.ones((CHUNK, CHUNK), dtype=bool), -1)
    decay = jnp.exp(jnp.where(tril, g[..., :, None] - g[..., None, :], -jnp.inf))
    k_beta = k * beta[..., None]
    lower = jnp.where(strict, jnp.einsum("bhncd,bhnsd->bhncs", k_beta, k) * decay, 0.0)
    eye = jnp.eye(CHUNK, dtype=q.dtype)
    t_inv = lax.linalg.triangular_solve(eye + lower, jnp.broadcast_to(eye, lower.shape),
                                        left_side=True, lower=True)
    u = jnp.einsum("bhncs,bhnse->bhnce", t_inv, v * beta[..., None])
    w = jnp.einsum("bhncs,bhnsd->bhncd", t_inv, k_beta * jnp.exp(g)[..., None])
    attn = jnp.where(tril, jnp.einsum("bhncd,bhnsd->bhncs", q, k) * decay, 0.0)
    q_dec = q * jnp.exp(g)[..., None]
    g_last = g[..., -1]
    k_dec = k * jnp.exp(g_last[..., None] - g)[..., None]

    def step(state, inp):
        q_i, w_i, u_i, a_i, k_i, gl_i = inp
        v_new = u_i - jnp.einsum("bhcd,bhde->bhce", w_i, state)
        o_i = jnp.einsum("bhcd,bhde->bhce", q_i, state) + jnp.einsum("bhcs,bhse->bhce", a_i, v_new)
        state = state * jnp.exp(gl_i)[..., None, None] + jnp.einsum("bhcd,bhce->bhde", k_i, v_new)
        return state, o_i

    xs = tuple(jnp.moveaxis(t, 2, 0) for t in (q_dec, w, u, attn, k_dec, g_last))
    state0 = jnp.zeros((B, H, DK, DV), dtype=q.dtype)
    _, o = lax.scan(step, state0, xs)
    o = jnp.moveaxis(o, 0, 2).reshape(B, H, T, DV)
    return jnp.moveaxis(o, 1, 2)


def _layer(x, norm_w, w_in, conv_w, a_log, dt_bias, o_norm_w, w_pa, w_pb, w_o):
    B, T, _ = x.shape
    xn = _rmsnorm(x, norm_w)
    proj = xn @ w_in.astype(x.dtype)
    sizes = [QK_W, QK_W, V_W, V_W, N_DIR * N_HEADS, N_DIR * N_HEADS, F_W, F_W, D_MODEL, D_MODEL]
    idx = [int(s) for s in np.cumsum(sizes)[:-1]]
    q, k, v, z, a, b, uf, zf, ga, gb = jnp.split(proj, idx, axis=-1)

    qkv = jax.nn.silu(_centred_dwconv(jnp.concatenate([q, k, v], axis=-1), conv_w))
    q, k, v = jnp.split(qkv.astype(jnp.float32), [QK_W, 2 * QK_W], axis=-1)
    q = _l2norm(q.reshape(B, T, N_HEADS, HEAD_DK)) * (HEAD_DK ** -0.5)
    k = _l2norm(k.reshape(B, T, N_HEADS, HEAD_DK))
    v = v.reshape(B, T, N_HEADS, HEAD_DV)
    a = a.astype(jnp.float32).reshape(B, T, N_DIR, N_HEADS)
    b = b.astype(jnp.float32).reshape(B, T, N_DIR, N_HEADS)
    g = -jnp.exp(a_log.astype(jnp.float32)) * jax.nn.softplus(a + dt_bias.astype(jnp.float32))
    beta = jax.nn.sigmoid(b)
    o_fwd = _gated_delta_chunked(q, k, v, g[:, :, 0], beta[:, :, 0])
    flip = lambda t: jnp.flip(t, axis=1)
    o_bwd = flip(_gated_delta_chunked(flip(q), flip(k), flip(v), flip(g[:, :, 1]), flip(beta[:, :, 1])))
    o = o_fwd + o_bwd
    o = o * lax.rsqrt(jnp.mean(o * o, axis=-1, keepdims=True) + EPS) * o_norm_w.astype(jnp.float32)
    o = o * jax.nn.silu(z.astype(jnp.float32).reshape(B, T, N_HEADS, HEAD_DV))
    y_a = o.reshape(B, T, V_W).astype(x.dtype) @ w_pa.astype(x.dtype)

    uf = uf.astype(jnp.float32).reshape(B, T, F_GROUPS, F_GROUP_W)
    f = jnp.fft.fft2(uf, axes=(1, 3), norm="ortho").real.reshape(B, T, F_W)
    f = f * jax.nn.silu(zf.astype(jnp.float32))
    y_b = f.astype(x.dtype) @ w_pb.astype(x.dtype)

    merged = jax.nn.sigmoid(ga) * y_a + jax.nn.sigmoid(gb) * y_b
    return x + merged @ w_o.astype(x.dtype)


def _trunk(x, norm_w, w_in, conv_w, a_log, dt_bias, o_norm_w, w_pa, w_pb, w_o, final_norm_w):
    for l in range(DEPTH):
        x = _layer(x, norm_w[l], w_in[l], conv_w[l], a_log[l], dt_bias[l], o_norm_w[l],
                   w_pa[l], w_pb[l], w_o[l])
    return _rmsnorm(x, final_norm_w)


def setup_inputs(seed: int = 0) -> dict:
    key = jax.random.key(seed)
    ks = jax.random.split(key, 14)
    f32 = jnp.float32
    x_prompt = jax.random.normal(ks[0], (BATCH, SEQ, D_MODEL), f32)
    x_sample = jax.random.normal(ks[1], (DEC_BATCH, DEC_SEQ, D_MODEL), f32)
    norm_w = 1.0 + 0.02 * jax.random.normal(ks[2], (DEPTH, D_MODEL), f32)
    w_in = jax.random.normal(ks[3], (DEPTH, D_MODEL, IN_W), f32) * D_MODEL ** -0.5
    conv_w = jax.random.normal(ks[4], (DEPTH, CONV_K, 2 * QK_W + V_W), f32) * CONV_K ** -0.5
    a_log = jnp.log(jax.random.uniform(ks[5], (DEPTH, N_DIR, N_HEADS), f32, 1.0, 16.0))
    dt = jnp.exp(jax.random.uniform(ks[6], (DEPTH, N_DIR, N_HEADS), f32,
                                    math.log(1e-3), math.log(1e-1)))
    dt_bias = dt + jnp.log(-jnp.expm1(-dt))
    o_norm_w = 1.0 + 0.02 * jax.random.normal(ks[7], (DEPTH, HEAD_DV), f32)
    w_pa = jax.random.normal(ks[8], (DEPTH, V_W, D_MODEL), f32) * V_W ** -0.5
    w_pb = jax.random.normal(ks[9], (DEPTH, F_W, D_MODEL), f32) * F_W ** -0.5
    w_o = jax.random.normal(ks[10], (DEPTH, D_MODEL, D_MODEL), f32) * D_MODEL ** -0.5
    final_norm_w = 1.0 + 0.02 * jax.random.normal(ks[11], (D_MODEL,), f32)
    return {"x_prompt": x_prompt, "x_sample": x_sample, "norm_w": norm_w, "w_in": w_in,
            "conv_w": conv_w, "a_log": a_log, "dt_bias": dt_bias, "o_norm_w": o_norm_w,
            "w_pa": w_pa, "w_pb": w_pb, "w_o": w_o, "final_norm_w": final_norm_w}


def reference(x_prompt, x_sample, norm_w, w_in, conv_w, a_log, dt_bias, o_norm_w, w_pa, w_pb,
              w_o, final_norm_w):
    y_prompt = _trunk(x_prompt, norm_w, w_in, conv_w, a_log, dt_bias, o_norm_w, w_pa, w_pb,
                      w_o, final_norm_w)
    y_sample = _trunk(x_sample, norm_w, w_in, conv_w, a_log, dt_bias, o_norm_w, w_pa, w_pb,
                      w_o, final_norm_w)
    return (y_prompt, y_sample)
```

```python
import functools
import math

import numpy as np
import jax
import jax.numpy as jnp
from jax import lax
from jax.experimental import pallas as pl
from jax.experimental.pallas import tpu as pltpu

D_MODEL = 1024
N_HEADS = 8
HEAD_D = 128
QK_W = N_HEADS * HEAD_D
CONV_K = 5
CONV_PAD = CONV_K // 2
N_DIR = 2
F_GROUPS = 4
F_GROUP_W = 128
F_W = F_GROUPS * F_GROUP_W
EPS = 1e-6

LANES = 128
SUBLANES = 8
DELTA_CHUNK = 128
DELTA_LEVELS = 7
ROW_BLOCK = 512
FFT_COL_T2 = 4
FFT_ROW_T1 = 4
VMEM_LIMIT = 56 * 1024 * 1024

G_LANE = 0
B_LANE = N_DIR * N_HEADS

BF16 = jnp.bfloat16
F32 = jnp.float32


def _dot(a, b):
    return jnp.dot(a, b, preferred_element_type=F32)


def _silu(x):
    return x / (1.0 + jnp.exp(-x))


def _sigmoid(x):
    return 1.0 / (1.0 + jnp.exp(-x))


def _in_proj_kernel(x_ref, xp_ref, xn_ref, nw_ref, wqkv_ref, wrest_ref, wabh_ref, wabl_ref,
                    convw_ref, alog_ref, dtb_ref,
                    q_ref, k_ref, v_ref, sz_ref, gate_ref, uf_ref, szf_ref, sga_ref, sgb_ref,
                    pext_ref):
    i = pl.program_id(1)
    n = pl.num_programs(1)
    tb = x_ref.shape[0]
    nw = nw_ref[...]

    def norm(xb):
        ms = jnp.mean(xb * xb, axis=-1, keepdims=True)
        return xb * lax.rsqrt(ms + EPS) * nw

    xm = norm(x_ref[...])
    xp = norm(xp_ref[...]) * jnp.where(i > 0, 1.0, 0.0)
    xn = norm(xn_ref[...]) * jnp.where(i < n - 1, 1.0, 0.0)
    xm_bf = xm.astype(BF16)
    xext_bf = jnp.concatenate([xp, xm, xn], axis=0).astype(BF16)

    ones = jnp.ones((HEAD_D, HEAD_D), BF16)
    row0 = SUBLANES - CONV_PAD
    for seg, out_ref in enumerate((q_ref, k_ref, v_ref)):
        pext_ref[...] = _dot(xext_bf, wqkv_ref[:, seg * QK_W:(seg + 1) * QK_W])
        for h in range(N_HEADS):
            cols = slice(h * HEAD_D, (h + 1) * HEAD_D)
            wcols = slice(seg * QK_W + h * HEAD_D, seg * QK_W + (h + 1) * HEAD_D)
            acc = None
            for j in range(CONV_K):
                term = convw_ref[j:j + 1, wcols] * pext_ref[pl.ds(row0 + j, tb), cols]
                acc = term if acc is None else acc + term
            s = _silu(acc)
            if seg < 2:
                ss = _dot((s * s).astype(BF16), ones)
                s = s * lax.rsqrt(ss + EPS)
                if seg == 0:
                    s = s * (HEAD_D ** -0.5)
            out_ref[:, cols] = s.astype(out_ref.dtype)

    half = F_W
    sz_ref[:, :half] = _silu(_dot(xm_bf, wrest_ref[:, 0:half])).astype(BF16)
    sz_ref[:, half:] = _silu(_dot(xm_bf, wrest_ref[:, half:2 * half])).astype(BF16)
    uf_ref[...] = _dot(xm_bf, wrest_ref[:, 2 * half:3 * half]).astype(BF16)
    szf_ref[...] = _silu(_dot(xm_bf, wrest_ref[:, 3 * half:4 * half])).astype(BF16)
    sga_ref[:, :half] = _sigmoid(_dot(xm_bf, wrest_ref[:, 4 * half:5 * half])).astype(BF16)
    sga_ref[:, half:] = _sigmoid(_dot(xm_bf, wrest_ref[:, 5 * half:6 * half])).astype(BF16)
    sgb_ref[:, :half] = _sigmoid(_dot(xm_bf, wrest_ref[:, 6 * half:7 * half])).astype(BF16)
    sgb_ref[:, half:] = _sigmoid(_dot(xm_bf, wrest_ref[:, 7 * half:8 * half])).astype(BF16)

    xm_lo = (xm - xm_bf.astype(F32)).astype(BF16)
    wh = wabh_ref[...]
    ab = _dot(xm_bf, wh) + _dot(xm_lo, wh) + _dot(xm_bf, wabl_ref[...])
    lane = lax.broadcasted_iota(jnp.int32, ab.shape, 1)
    xs = ab + dtb_ref[...]
    softplus = jnp.maximum(xs, 0.0) + jnp.log(1.0 + jnp.exp(-jnp.abs(xs)))
    g = jnp.where(lane < B_LANE, -jnp.exp(alog_ref[...]) * softplus, 0.0)
    beta = _sigmoid(ab)

    c = DELTA_CHUNK
    r = lax.broadcasted_iota(jnp.int32, (c, c), 0)
    cidx = lax.broadcasted_iota(jnp.int32, (c, c), 1)
    tril = jnp.where(r >= cidx, 1.0, 0.0).astype(F32)
    triu = jnp.where(r <= cidx, 1.0, 0.0).astype(F32)
    ln = lax.broadcasted_iota(jnp.int32, (c, LANES), 1)
    for ci in range(tb // c):
        rows = slice(ci * c, (ci + 1) * c)
        gch = g[rows]
        cf = jnp.dot(tril, gch, precision=lax.Precision.HIGHEST, preferred_element_type=F32)
        cb = jnp.dot(triu, gch, precision=lax.Precision.HIGHEST, preferred_element_type=F32)
        out = jnp.where(ln < N_HEADS, cf,
                        jnp.where(ln < B_LANE, cb,
                                  jnp.where(ln < 2 * B_LANE, beta[rows], 0.0)))
        gate_ref[rows, :] = out


def _in_proj(x, norm_w, wqkv, wrest, wabh, wabl, convw, alog, dtb):
    b, t, d = x.shape
    tb = min(ROW_BLOCK, t)
    nb = t // tb
    r8 = tb // SUBLANES
    last8 = t // SUBLANES - 1

    def rows(bi, i):
        return (bi, i, 0)

    def const(bi, i):
        return (0, 0)

    row_spec = lambda w: pl.BlockSpec((None, tb, w), rows)
    full = lambda a: pl.BlockSpec(a.shape, const)
    out_shapes = (
        jax.ShapeDtypeStruct((b, t, QK_W), BF16),
        jax.ShapeDtypeStruct((b, t, QK_W), BF16),
        jax.ShapeDtypeStruct((b, t, QK_W), BF16),
        jax.ShapeDtypeStruct((b, t, QK_W), BF16),
        jax.ShapeDtypeStruct((b, t, LANES), F32),
        jax.ShapeDtypeStruct((b, t, F_W), BF16),
        jax.ShapeDtypeStruct((b, t, F_W), BF16),
        jax.ShapeDtypeStruct((b, t, D_MODEL), BF16),
        jax.ShapeDtypeStruct((b, t, D_MODEL), BF16),
    )
    return pl.pallas_call(
        _in_proj_kernel,
        grid=(b, nb),
        in_specs=[
            row_spec(d),
            pl.BlockSpec((None, SUBLANES, d), lambda bi, i: (bi, jnp.maximum(i * r8 - 1, 0), 0)),
            pl.BlockSpec((None, SUBLANES, d), lambda bi, i: (bi, jnp.minimum((i + 1) * r8, last8), 0)),
            full(norm_w), full(wqkv), full(wrest), full(wabh), full(wabl), full(convw),
            full(alog), full(dtb),
        ],
        out_specs=[row_spec(s.shape[-1]) for s in out_shapes],
        out_shape=out_shapes,
        scratch_shapes=[pltpu.VMEM((tb + 2 * SUBLANES, QK_W), F32)],
        compiler_params=pltpu.CompilerParams(
            dimension_semantics=("parallel", "parallel"), vmem_limit_bytes=VMEM_LIMIT),
        name="in_proj",
    )(x, x, x, norm_w, wqkv, wrest, wabh, wabl, convw, alog, dtb)


@functools.lru_cache(maxsize=None)
def _fft_tables(n1, n2):
    t = n1 * n2
    c = np.arange(F_GROUP_W)
    ang = 2.0 * np.pi * ((c[:, None] * c[None, :]) % F_GROUP_W) / F_GROUP_W
    sc = 1.0 / math.sqrt(F_GROUP_W)
    wc = np.concatenate([np.cos(ang), -np.sin(ang)], axis=1) * sc
    a = np.arange(n1)
    ang1 = 2.0 * np.pi * ((a[:, None] * a[None, :]) % n1) / n1
    fr, fi = np.cos(ang1) / math.sqrt(n1), -np.sin(ang1) / math.sqrt(n1)
    m1 = np.block([[fr, -fi], [fi, fr]])
    t1p = np.arange(n1)[:, None, None]
    t2p = np.arange(n2)[None, :, None]
    t2 = np.arange(n2)[None, None, :]
    ang2 = 2.0 * np.pi * ((t2 * (t1p + n1 * t2p)) % t) / t
    g = np.concatenate([np.cos(ang2), np.sin(ang2)], axis=2) / math.sqrt(n2)
    as_bf = lambda m: m.astype(np.float32).astype(BF16)
    return as_bf(wc), as_bf(m1), as_bf(g)


def _fft_a_kernel(u_ref, wc_ref, m1_ref, yr_ref, yi_ref):
    n1 = u_ref.shape[0]
    nsl = u_ref.shape[1] // F_GROUP_W
    ub = u_ref[...]
    wc = wc_ref[...]
    zr, zi = [], []
    for s in range(nsl):
        z = _dot(ub[:, s * F_GROUP_W:(s + 1) * F_GROUP_W], wc)
        zr.append(z[:, :F_GROUP_W])
        zi.append(z[:, F_GROUP_W:])
    zst = jnp.concatenate([jnp.concatenate(zr, axis=1), jnp.concatenate(zi, axis=1)], axis=0)
    y = _dot(m1_ref[...], zst.astype(BF16))
    yr_ref[...] = y[:n1].astype(BF16)
    yi_ref[...] = y[n1:].astype(BF16)


def _fft_b_kernel(yr_ref, yi_ref, g_ref, f_ref):
    n2 = g_ref.shape[1]
    for j in range(g_ref.shape[0]):
        rows = slice(j * n2, (j + 1) * n2)
        yst = jnp.concatenate([yr_ref[rows, :], yi_ref[rows, :]], axis=0)
        f_ref[:, j * F_W:(j + 1) * F_W] = _dot(g_ref[j], yst)


def _fft_real(uf):
    b, t, _ = uf.shape
    n1 = 1 << ((t.bit_length() - 1) // 2)
    n2 = t // n1
    wc, m1, g = _fft_tables(n1, n2)
    ct2 = min(FFT_COL_T2, n2)
    rt1 = min(FFT_ROW_T1, n1)
    u2 = uf.reshape(b, n1, n2 * F_W)
    ycols = ct2 * F_W
    yr, yi = pl.pallas_call(
        _fft_a_kernel,
        grid=(b, n2 // ct2),
        in_specs=[pl.BlockSpec((None, n1, ycols), lambda bi, i: (bi, 0, i)),
                  pl.BlockSpec(wc.shape, lambda bi, i: (0, 0)),
                  pl.BlockSpec(m1.shape, lambda bi, i: (0, 0))],
        out_specs=[pl.BlockSpec((None, n1, ycols), lambda bi, i: (bi, 0, i))] * 2,
        out_shape=(jax.ShapeDtypeStruct((b, n1, n2 * F_W), BF16),) * 2,
        compiler_params=pltpu.CompilerParams(dimension_semantics=("parallel", "parallel")),
        name="fft_a",
    )(u2, wc, m1)
    yr = yr.reshape(b, t, F_W)
    yi = yi.reshape(b, t, F_W)
    f = pl.pallas_call(
        _fft_b_kernel,
        grid=(b, n1 // rt1),
        in_specs=[pl.BlockSpec((None, rt1 * n2, F_W), lambda bi, i: (bi, i, 0)),
                  pl.BlockSpec((None, rt1 * n2, F_W), lambda bi, i: (bi, i, 0)),
                  pl.BlockSpec((rt1, n2, 2 * n2), lambda bi, i: (i, 0, 0))],
        out_specs=pl.BlockSpec((None, n2, rt1 * F_W), lambda bi, i: (bi, 0, i)),
        out_shape=jax.ShapeDtypeStruct((b, n2, n1 * F_W), F32),
        compiler_params=pltpu.CompilerParams(dimension_semantics=("parallel", "parallel")),
        name="fft_b",
    )(yr, yi, g)
    return f.reshape(b, t, F_W)


def _delta_kernel(*refs, reverse, finish):
    if finish:
        q_ref, k_ref, v_ref, gate_ref, ofwd_ref, sz_ref, onw_ref, o_ref, s_ref = refs
    else:
        q_ref, k_ref, v_ref, gate_ref, o_ref, s_ref = refs
    c = DELTA_CHUNK

    @pl.when(pl.program_id(1) == 0)
    def _():
        s_ref[...] = jnp.zeros_like(s_ref)

    gcolumns = gate_ref[...]
    grows = gcolumns.T
    row = lax.broadcasted_iota(jnp.int32, (c, c), 0)
    col = lax.broadcasted_iota(jnp.int32, (c, c), 1)
    if reverse:
        valid, strict, last = row <= col, row < col, 0
    else:
        valid, strict, last = row >= col, row > col, c - 1
    eye = jnp.where(row == col, 1.0, 0.0).astype(F32)
    level_masks = [((row >> l) ^ (col >> l)) == 1 for l in range(DELTA_LEVELS)]
    ones = jnp.ones((HEAD_D, HEAD_D), BF16)
    nt_dims = (((1,), (1,)), ((), ()))
    tn_dims = (((0,), (0,)), ((), ()))

    for h in range(N_HEADS):
        cols = slice(h * HEAD_D, (h + 1) * HEAD_D)
        gl = G_LANE + (N_HEADS if reverse else 0) + h
        bl = B_LANE + (N_HEADS if reverse else 0) + h
        g_i = jnp.broadcast_to(gcolumns[:, gl:gl + 1], (c, c))
        g_j = jnp.broadcast_to(grows[gl:gl + 1, :], (c, c))
        b_i = jnp.broadcast_to(gcolumns[:, bl:bl + 1], (c, c))
        b_j = jnp.broadcast_to(grows[bl:bl + 1, :], (c, c))
        g_last = jnp.broadcast_to(grows[gl:gl + 1, last:last + 1], (c, c))
        decay = jnp.exp(jnp.where(valid, g_i - g_j, -1e30))

        q = q_ref[:, cols]
        k = k_ref[:, cols]
        v = v_ref[:, cols]
        gram = lax.dot_general(jnp.concatenate([q, k], axis=0), k, nt_dims,
                               preferred_element_type=F32)
        attn = gram[:c] * decay
        lmat = jnp.where(strict, b_i * gram[c:] * decay, 0.0)

        x = eye - jnp.where(level_masks[0], lmat, 0.0)
        for l in range(1, DELTA_LEVELS):
            cl = jnp.where(level_masks[l], lmat, 0.0).astype(BF16)
            xb = x.astype(BF16)
            x = x - _dot(xb, _dot(cl, xb).astype(BF16))

        tb_ = x * b_j
        u = _dot(tb_.astype(BF16), v)
        w = _dot((tb_ * jnp.exp(g_j)).astype(BF16), k)

        s = s_ref[h]
        wq = _dot(jnp.concatenate([w.astype(BF16), q], axis=0), s.astype(BF16))
        v_new = u - wq[:c]
        v_new_bf = v_new.astype(BF16)
        o = jnp.exp(g_i) * wq[c:] + _dot(attn.astype(BF16), v_new_bf)
        k_dec = (k.astype(F32) * jnp.exp(g_last - g_i)).astype(BF16)
        s_ref[h] = s * jnp.exp(g_last) + lax.dot_general(k_dec, v_new_bf, tn_dims,
                                                          preferred_element_type=F32)
        if finish:
            o = o + ofwd_ref[:, cols]
            ms = _dot((o * o).astype(BF16), ones) * (1.0 / HEAD_D)
            o = o * lax.rsqrt(ms + EPS) * onw_ref[...] * sz_ref[:, cols].astype(F32)
        o_ref[:, cols] = o.astype(o_ref.dtype)


def _delta(q, k, v, gates, reverse, ofwd=None, sz=None, onw=None):
    b, t, _ = q.shape
    c = DELTA_CHUNK
    nc = t // c
    finish = ofwd is not None
    if reverse:
        rows = lambda bi, i: (bi, nc - 1 - i, 0)
    else:
        rows = lambda bi, i: (bi, i, 0)
    wide = pl.BlockSpec((None, c, QK_W), rows)
    in_specs = [wide, wide, wide, pl.BlockSpec((None, c, LANES), rows)]
    args = [q, k, v, gates]
    if finish:
        in_specs += [wide, wide, pl.BlockSpec(onw.shape, lambda bi, i: (0, 0))]
        args += [ofwd, sz, onw]
    return pl.pallas_call(
        functools.partial(_delta_kernel, reverse=reverse, finish=finish),
        grid=(b, nc),
        in_specs=in_specs,
        out_specs=wide,
        out_shape=jax.ShapeDtypeStruct((b, t, QK_W), BF16 if finish else F32),
        scratch_shapes=[pltpu.VMEM((N_HEADS, HEAD_D, HEAD_D), F32)],
        compiler_params=pltpu.CompilerParams(dimension_semantics=("arbitrary", "arbitrary")),
        name="delta_bwd" if reverse else "delta_fwd",
    )(*args)


def _out_proj_kernel(og_ref, f_ref, szf_ref, sga_ref, sgb_ref, x_ref, wpa_ref, wpb_ref, wo_ref,
                     fnw_ref, o_ref, *, final):
    ya = _dot(og_ref[...], wpa_ref[...])
    fg = (f_ref[...] * szf_ref[...].astype(F32)).astype(BF16)
    yb = _dot(fg, wpb_ref[...])
    merged = sga_ref[...].astype(F32) * ya + sgb_ref[...].astype(F32) * yb
    xo = x_ref[...] + _dot(merged.astype(BF16), wo_ref[...])
    if final:
        ms = jnp.mean(xo * xo, axis=-1, keepdims=True)
        xo = xo * lax.rsqrt(ms + EPS) * fnw_ref[...]
    o_ref[...] = xo


def _out_proj(og, f, szf, sga, sgb, x, wpa, wpb, wo, fnw, final):
    b, t, d = x.shape
    tb = min(ROW_BLOCK, t)
    rows = lambda bi, i: (bi, i, 0)
    const = lambda bi, i: (0, 0)
    row_spec = lambda w: pl.BlockSpec((None, tb, w), rows)
    full = lambda a: pl.BlockSpec(a.shape, const)
    return pl.pallas_call(
        functools.partial(_out_proj_kernel, final=final),
        grid=(b, t // tb),
        in_specs=[row_spec(QK_W), row_spec(F_W), row_spec(F_W), row_spec(d), row_spec(d), row_spec(d),
                  full(wpa), full(wpb), full(wo), full(fnw)],
        out_specs=row_spec(d),
        out_shape=jax.ShapeDtypeStruct((b, t, d), F32),
        compiler_params=pltpu.CompilerParams(
            dimension_semantics=("parallel", "parallel"), vmem_limit_bytes=VMEM_LIMIT),
        name="out_proj",
    )(og, f, szf, sga, sgb, x, wpa, wpb, wo, fnw)


def _pad_lanes(v):
    flat = v.reshape(1, -1).astype(F32)
    return jnp.pad(flat, ((0, 0), (0, LANES - flat.shape[1])))


def _layer_params(norm_w, w_in, conv_w, a_log, dt_bias, o_norm_w, w_pa, w_pb, w_o):
    o_z = 3 * QK_W
    o_ab = o_z + QK_W
    o_uf = o_ab + 2 * N_DIR * N_HEADS
    wqkv = w_in[:, :o_z].astype(BF16)
    wrest = jnp.concatenate([w_in[:, o_z:o_ab], w_in[:, o_uf:]], axis=1).astype(BF16)
    wab = jnp.pad(w_in[:, o_ab:o_uf], ((0, 0), (0, LANES - 2 * N_DIR * N_HEADS)))
    wabh = wab.astype(BF16)
    wabl = (wab - wabh.astype(F32)).astype(BF16)
    convw = jnp.pad(conv_w, ((0, SUBLANES - CONV_K), (0, 0)))
    return dict(norm_w=norm_w.reshape(1, -1), wqkv=wqkv, wrest=wrest, wabh=wabh, wabl=wabl,
                convw=convw, alog=_pad_lanes(a_log), dtb=_pad_lanes(dt_bias),
                onw=o_norm_w.reshape(1, -1), wpa=w_pa.astype(BF16), wpb=w_pb.astype(BF16),
                wo=w_o.astype(BF16))


def _layer(x, p, fnw, final):
    q, k, v, sz, gates, uf, szf, sga, sgb = _in_proj(
        x, p["norm_w"], p["wqkv"], p["wrest"], p["wabh"], p["wabl"], p["convw"], p["alog"], p["dtb"])
    f = _fft_real(uf)
    o_fwd = _delta(q, k, v, gates, reverse=False)
    og = _delta(q, k, v, gates, reverse=True, ofwd=o_fwd, sz=sz, onw=p["onw"])
    return _out_proj(og, f, szf, sga, sgb, x, p["wpa"], p["wpb"], p["wo"], fnw, final)


def _trunk(x, params, fnw):
    depth = len(params)
    for l, p in enumerate(params):
        x = _layer(x, p, fnw, final=(l == depth - 1))
    return x


def kernel(x_prompt, x_sample, norm_w, w_in, conv_w, a_log, dt_bias, o_norm_w, w_pa, w_pb, w_o,
           final_norm_w):
    depth = norm_w.shape[0]
    params = [_layer_params(norm_w[l], w_in[l], conv_w[l], a_log[l], dt_bias[l], o_norm_w[l],
                            w_pa[l], w_pb[l], w_o[l]) for l in range(depth)]
    fnw = final_norm_w.reshape(1, -1)
    return (_trunk(x_prompt, params, fnw), _trunk(x_sample, params, fnw))
```

```python
import functools
import math

import numpy as np
import jax
import jax.numpy as jnp
from jax import lax
from jax.experimental import pallas as pl
from jax.experimental.pallas import tpu as pltpu

D_MODEL = 1024
N_HEADS = 8
HEAD_D = 128
QK_W = N_HEADS * HEAD_D
CONV_K = 5
CONV_PAD = CONV_K // 2
N_DIR = 2
F_GROUPS = 4
F_GROUP_W = 128
F_W = F_GROUPS * F_GROUP_W
EPS = 1e-6

LANES = 128
SUBLANES = 8
DELTA_CHUNK = 128
DELTA_LEVELS = 7
ROW_BLOCK = 512
FFT_COL_T2 = 4
FFT_ROW_T1 = 4
VMEM_LIMIT = 56 * 1024 * 1024

G_LANE = 0
B_LANE = N_DIR * N_HEADS

BF16 = jnp.bfloat16
F32 = jnp.float32


def _dot(a, b):
    return jnp.dot(a, b, preferred_element_type=F32)


def _silu(x):
    return x / (1.0 + jnp.exp(-x))


def _sigmoid(x):
    return 1.0 / (1.0 + jnp.exp(-x))


def _in_proj_kernel(x_ref, xp_ref, xn_ref, nw_ref, wqkv_ref, wrest_ref, wabh_ref, wabl_ref,
                    convw_ref, alog_ref, dtb_ref,
                    q_ref, k_ref, v_ref, sz_ref, gate_ref, uf_ref, szf_ref, sga_ref, sgb_ref,
                    pext_ref):
    i = pl.program_id(1)
    n = pl.num_programs(1)
    tb = x_ref.shape[0]
    nw = nw_ref[...]

    def norm(xb):
        ms = jnp.mean(xb * xb, axis=-1, keepdims=True)
        return xb * lax.rsqrt(ms + EPS) * nw

    xm = norm(x_ref[...])
    xp = norm(xp_ref[...]) * jnp.where(i > 0, 1.0, 0.0)
    xn = norm(xn_ref[...]) * jnp.where(i < n - 1, 1.0, 0.0)
    xm_bf = xm.astype(BF16)
    xext_bf = jnp.concatenate([xp, xm, xn], axis=0).astype(BF16)

    ones = jnp.ones((HEAD_D, HEAD_D), BF16)
    row0 = SUBLANES - CONV_PAD
    for seg, out_ref in enumerate((q_ref, k_ref, v_ref)):
        pext_ref[...] = _dot(xext_bf, wqkv_ref[:, seg * QK_W:(seg + 1) * QK_W])
        for h in range(N_HEADS):
            cols = slice(h * HEAD_D, (h + 1) * HEAD_D)
            wcols = slice(seg * QK_W + h * HEAD_D, seg * QK_W + (h + 1) * HEAD_D)
            acc = None
            for j in range(CONV_K):
                term = convw_ref[j:j + 1, wcols] * pext_ref[pl.ds(row0 + j, tb), cols]
                acc = term if acc is None else acc + term
            s = _silu(acc)
            if seg < 2:
                ss = _dot((s * s).astype(BF16), ones)
                s = s * lax.rsqrt(ss + EPS)
                if seg == 0:
                    s = s * (HEAD_D ** -0.5)
            out_ref[:, cols] = s.astype(out_ref.dtype)

    half = F_W
    sz_ref[:, :half] = _silu(_dot(xm_bf, wrest_ref[:, 0:half])).astype(BF16)
    sz_ref[:, half:] = _silu(_dot(xm_bf, wrest_ref[:, half:2 * half])).astype(BF16)
    uf_ref[...] = _dot(xm_bf, wrest_ref[:, 2 * half:3 * half]).astype(BF16)
    szf_ref[...] = _silu(_dot(xm_bf, wrest_ref[:, 3 * half:4 * half])).astype(BF16)
    sga_ref[:, :half] = _sigmoid(_dot(xm_bf, wrest_ref[:, 4 * half:5 * half])).astype(BF16)
    sga_ref[:, half:] = _sigmoid(_dot(xm_bf, wrest_ref[:, 5 * half:6 * half])).astype(BF16)
    sgb_ref[:, :half] = _sigmoid(_dot(xm_bf, wrest_ref[:, 6 * half:7 * half])).astype(BF16)
    sgb_ref[:, half:] = _sigmoid(_dot(xm_bf, wrest_ref[:, 7 * half:8 * half])).astype(BF16)

    xm_lo = (xm - xm_bf.astype(F32)).astype(BF16)
    wh = wabh_ref[...]
    ab = _dot(xm_bf, wh) + _dot(xm_lo, wh) + _dot(xm_bf, wabl_ref[...])
    lane = lax.broadcasted_iota(jnp.int32, ab.shape, 1)
    xs = ab + dtb_ref[...]
    softplus = jnp.maximum(xs, 0.0) + jnp.log(1.0 + jnp.exp(-jnp.abs(xs)))
    g = jnp.where(lane < B_LANE, -jnp.exp(alog_ref[...]) * softplus, 0.0)
    beta = _sigmoid(ab)

    c = DELTA_CHUNK
    r = lax.broadcasted_iota(jnp.int32, (c, c), 0)
    cidx = lax.broadcasted_iota(jnp.int32, (c, c), 1)
    tril = jnp.where(r >= cidx, 1.0, 0.0).astype(F32)
    triu = jnp.where(r <= cidx, 1.0, 0.0).astype(F32)
    ln = lax.broadcasted_iota(jnp.int32, (c, LANES), 1)
    for ci in range(tb // c):
        rows = slice(ci * c, (ci + 1) * c)
        gch = g[rows]
        cf = jnp.dot(tril, gch, precision=lax.Precision.HIGHEST, preferred_element_type=F32)
        cb = jnp.dot(triu, gch, precision=lax.Precision.HIGHEST, preferred_element_type=F32)
        out = jnp.where(ln < N_HEADS, cf,
                        jnp.where(ln < B_LANE, cb,
                                  jnp.where(ln < 2 * B_LANE, beta[rows], 0.0)))
        gate_ref[rows, :] = out


def _in_proj(x, norm_w, wqkv, wrest, wabh, wabl, convw, alog, dtb):
    b, t, d = x.shape
    tb = min(ROW_BLOCK, t)
    nb = t // tb
    r8 = tb // SUBLANES
    last8 = t // SUBLANES - 1

    def rows(bi, i):
        return (bi, i, 0)

    def const(bi, i):
        return (0, 0)

    row_spec = lambda w: pl.BlockSpec((None, tb, w), rows)
    full = lambda a: pl.BlockSpec(a.shape, const)
    out_shapes = (
        jax.ShapeDtypeStruct((b, t, QK_W), BF16),
        jax.ShapeDtypeStruct((b, t, QK_W), BF16),
        jax.ShapeDtypeStruct((b, t, QK_W), BF16),
        jax.ShapeDtypeStruct((b, t, QK_W), BF16),
        jax.ShapeDtypeStruct((b, t, LANES), F32),
        jax.ShapeDtypeStruct((b, t, F_W), BF16),
        jax.ShapeDtypeStruct((b, t, F_W), BF16),
        jax.ShapeDtypeStruct((b, t, D_MODEL), BF16),
        jax.ShapeDtypeStruct((b, t, D_MODEL), BF16),
    )
    return pl.pallas_call(
        _in_proj_kernel,
        grid=(b, nb),
        in_specs=[
            row_spec(d),
            pl.BlockSpec((None, SUBLANES, d), lambda bi, i: (bi, jnp.maximum(i * r8 - 1, 0), 0)),
            pl.BlockSpec((None, SUBLANES, d), lambda bi, i: (bi, jnp.minimum((i + 1) * r8, last8), 0)),
            full(norm_w), full(wqkv), full(wrest), full(wabh), full(wabl), full(convw),
            full(alog), full(dtb),
        ],
        out_specs=[row_spec(s.shape[-1]) for s in out_shapes],
        out_shape=out_shapes,
        scratch_shapes=[pltpu.VMEM((tb + 2 * SUBLANES, QK_W), F32)],
        compiler_params=pltpu.CompilerParams(
            dimension_semantics=("parallel", "parallel"), vmem_limit_bytes=VMEM_LIMIT),
        name="in_proj",
    )(x, x, x, norm_w, wqkv, wrest, wabh, wabl, convw, alog, dtb)


@functools.lru_cache(maxsize=None)
def _fft_tables(n1, n2):
    t = n1 * n2
    c = np.arange(F_GROUP_W)
    ang = 2.0 * np.pi * ((c[:, None] * c[None, :]) % F_GROUP_W) / F_GROUP_W
    sc = 1.0 / math.sqrt(F_GROUP_W)
    wc = np.concatenate([np.cos(ang), -np.sin(ang)], axis=1) * sc
    a = np.arange(n1)
    ang1 = 2.0 * np.pi * ((a[:, None] * a[None, :]) % n1) / n1
    fr, fi = np.cos(ang1) / math.sqrt(n1), -np.sin(ang1) / math.sqrt(n1)
    m1 = np.block([[fr, -fi], [fi, fr]])
    t1p = np.arange(n1)[:, None, None]
    t2p = np.arange(n2)[None, :, None]
    t2 = np.arange(n2)[None, None, :]
    ang2 = 2.0 * np.pi * ((t2 * (t1p + n1 * t2p)) % t) / t
    g = np.concatenate([np.cos(ang2), np.sin(ang2)], axis=2) / math.sqrt(n2)
    as_bf = lambda m: m.astype(np.float32).astype(BF16)
    return as_bf(wc), as_bf(m1), as_bf(g)


def _fft_a_kernel(u_ref, wc_ref, m1_ref, yr_ref, yi_ref):
    n1 = u_ref.shape[0]
    nsl = u_ref.shape[1] // F_GROUP_W
    ub = u_ref[...]
    wc = wc_ref[...]
    zr, zi = [], []
    for s in range(nsl):
        z = _dot(ub[:, s * F_GROUP_W:(s + 1) * F_GROUP_W], wc)
        zr.append(z[:, :F_GROUP_W])
        zi.append(z[:, F_GROUP_W:])
    zst = jnp.concatenate([jnp.concatenate(zr, axis=1), jnp.concatenate(zi, axis=1)], axis=0)
    y = _dot(m1_ref[...], zst.astype(BF16))
    yr_ref[...] = y[:n1].astype(BF16)
    yi_ref[...] = y[n1:].astype(BF16)


def _fft_b_kernel(yr_ref, yi_ref, g_ref, f_ref):
    n2 = g_ref.shape[1]
    for j in range(g_ref.shape[0]):
        rows = slice(j * n2, (j + 1) * n2)
        yst = jnp.concatenate([yr_ref[rows, :], yi_ref[rows, :]], axis=0)
        f_ref[:, j * F_W:(j + 1) * F_W] = _dot(g_ref[j], yst)


def _fft_real(uf):
    b, t, _ = uf.shape
    n1 = 1 << ((t.bit_length() - 1) // 2)
    n2 = t // n1
    wc, m1, g = _fft_tables(n1, n2)
    ct2 = min(FFT_COL_T2, n2)
    rt1 = min(FFT_ROW_T1, n1)
    u2 = uf.reshape(b, n1, n2 * F_W)
    ycols = ct2 * F_W
    yr, yi = pl.pallas_call(
        _fft_a_kernel,
        grid=(b, n2 // ct2),
        in_specs=[pl.BlockSpec((None, n1, ycols), lambda bi, i: (bi, 0, i)),
                  pl.BlockSpec(wc.shape, lambda bi, i: (0, 0)),
                  pl.BlockSpec(m1.shape, lambda bi, i: (0, 0))],
        out_specs=[pl.BlockSpec((None, n1, ycols), lambda bi, i: (bi, 0, i))] * 2,
        out_shape=(jax.ShapeDtypeStruct((b, n1, n2 * F_W), BF16),) * 2,
        compiler_params=pltpu.CompilerParams(dimension_semantics=("parallel", "parallel")),
        name="fft_a",
    )(u2, wc, m1)
    yr = yr.reshape(b, t, F_W)
    yi = yi.reshape(b, t, F_W)
    f = pl.pallas_call(
        _fft_b_kernel,
        grid=(b, n1 // rt1),
        in_specs=[pl.BlockSpec((None, rt1 * n2, F_W), lambda bi, i: (bi, i, 0)),
                  pl.BlockSpec((None, rt1 * n2, F_W), lambda bi, i: (bi, i, 0)),
                  pl.BlockSpec((rt1, n2, 2 * n2), lambda bi, i: (i, 0, 0))],
        out_specs=pl.BlockSpec((None, n2, rt1 * F_W), lambda bi, i: (bi, 0, i)),
        out_shape=jax.ShapeDtypeStruct((b, n2, n1 * F_W), F32),
        compiler_params=pltpu.CompilerParams(dimension_semantics=("parallel", "parallel")),
        name="fft_b",
    )(yr, yi, g)
    return f.reshape(b, t, F_W)


def _delta_kernel(*refs, reverse, finish):
    if finish:
        q_ref, k_ref, v_ref, gate_ref, ofwd_ref, sz_ref, onw_ref, o_ref, s_ref = refs
    else:
        q_ref, k_ref, v_ref, gate_ref, o_ref, s_ref = refs
    c = DELTA_CHUNK

    @pl.when(pl.program_id(1) == 0)
    def _():
        s_ref[...] = jnp.zeros_like(s_ref)

    gcolumns = gate_ref[...]
    grows = gcolumns.T
    row = lax.broadcasted_iota(jnp.int32, (c, c), 0)
    col = lax.broadcasted_iota(jnp.int32, (c, c), 1)
    if reverse:
        valid, strict, last = row <= col, row < col, 0
    else:
        valid, strict, last = row >= col, row > col, c - 1
    eye = jnp.where(row == col, 1.0, 0.0).astype(F32)
    level_masks = [((row >> l) ^ (col >> l)) == 1 for l in range(DELTA_LEVELS)]
    ones = jnp.ones((HEAD_D, HEAD_D), BF16)
    nt_dims = (((1,), (1,)), ((), ()))
    tn_dims = (((0,), (0,)), ((), ()))

    heads = range(N_HEADS)
    cols = [slice(h * HEAD_D, (h + 1) * HEAD_D) for h in heads]
    dir_off = N_HEADS if reverse else 0
    g_i, g_j, b_j, g_last, lmat, attn = [], [], [], [], [], []
    for h in heads:
        gl = G_LANE + dir_off + h
        bl = B_LANE + dir_off + h
        g_i.append(jnp.broadcast_to(gcolumns[:, gl:gl + 1], (c, c)))
        g_j.append(jnp.broadcast_to(grows[gl:gl + 1, :], (c, c)))
        b_i = jnp.broadcast_to(gcolumns[:, bl:bl + 1], (c, c))
        b_j.append(jnp.broadcast_to(grows[bl:bl + 1, :], (c, c)))
        g_last.append(jnp.broadcast_to(grows[gl:gl + 1, last:last + 1], (c, c)))
        decay = jnp.exp(jnp.where(valid, g_i[h] - g_j[h], -1e30))
        qk = jnp.concatenate([q_ref[:, cols[h]], k_ref[:, cols[h]]], axis=0)
        gram = lax.dot_general(qk, k_ref[:, cols[h]], nt_dims,
                               preferred_element_type=F32)
        attn.append((gram[:c] * decay).astype(BF16))
        lmat.append(jnp.where(strict, b_i * gram[c:] * decay, 0.0))

    x = [eye - jnp.where(level_masks[0], lmat[h], 0.0) for h in heads]
    for l in range(1, DELTA_LEVELS):
        xb = [x[h].astype(BF16) for h in heads]
        cx = [_dot(jnp.where(level_masks[l], lmat[h], 0.0).astype(BF16), xb[h]).astype(BF16)
              for h in heads]
        x = [x[h] - _dot(xb[h], cx[h]) for h in heads]

    tb_ = [x[h] * b_j[h] for h in heads]
    u = [_dot(tb_[h].astype(BF16), v_ref[:, cols[h]]) for h in heads]
    w = [_dot((tb_[h] * jnp.exp(g_j[h])).astype(BF16), k_ref[:, cols[h]]).astype(BF16)
         for h in heads]

    wq = [_dot(jnp.concatenate([w[h], q_ref[:, cols[h]]], axis=0), s_ref[h].astype(BF16))
          for h in heads]
    v_new = [(u[h] - wq[h][:c]).astype(BF16) for h in heads]
    for h in heads:
        k_dec = (k_ref[:, cols[h]].astype(F32) * jnp.exp(g_last[h] - g_i[h])).astype(BF16)
        s_ref[h] = s_ref[h] * jnp.exp(g_last[h]) + lax.dot_general(
            k_dec, v_new[h], tn_dims, preferred_element_type=F32)
    for h in heads:
        o = jnp.exp(g_i[h]) * wq[h][c:] + _dot(attn[h], v_new[h])
        if finish:
            o = o + ofwd_ref[:, cols[h]]
            ms = _dot((o * o).astype(BF16), ones) * (1.0 / HEAD_D)
            o = o * lax.rsqrt(ms + EPS) * onw_ref[...] * sz_ref[:, cols[h]].astype(F32)
        o_ref[:, cols[h]] = o.astype(o_ref.dtype)


def _delta(q, k, v, gates, reverse, ofwd=None, sz=None, onw=None):
    b, t, _ = q.shape
    c = DELTA_CHUNK
    nc = t // c
    finish = ofwd is not None
    if reverse:
        rows = lambda bi, i: (bi, nc - 1 - i, 0)
    else:
        rows = lambda bi, i: (bi, i, 0)
    wide = pl.BlockSpec((None, c, QK_W), rows)
    in_specs = [wide, wide, wide, pl.BlockSpec((None, c, LANES), rows)]
    args = [q, k, v, gates]
    if finish:
        in_specs += [wide, wide, pl.BlockSpec(onw.shape, lambda bi, i: (0, 0))]
        args += [ofwd, sz, onw]
    return pl.pallas_call(
        functools.partial(_delta_kernel, reverse=reverse, finish=finish),
        grid=(b, nc),
        in_specs=in_specs,
        out_specs=wide,
        out_shape=jax.ShapeDtypeStruct((b, t, QK_W), BF16 if finish else F32),
        scratch_shapes=[pltpu.VMEM((N_HEADS, HEAD_D, HEAD_D), F32)],
        compiler_params=pltpu.CompilerParams(dimension_semantics=("arbitrary", "arbitrary")),
        name="delta_bwd" if reverse else "delta_fwd",
    )(*args)


def _out_proj_kernel(og_ref, f_ref, szf_ref, sga_ref, sgb_ref, x_ref, wpa_ref, wpb_ref, wo_ref,
                     fnw_ref, o_ref, *, final):
    ya = _dot(og_ref[...], wpa_ref[...])
    fg = (f_ref[...] * szf_ref[...].astype(F32)).astype(BF16)
    yb = _dot(fg, wpb_ref[...])
    merged = sga_ref[...].astype(F32) * ya + sgb_ref[...].astype(F32) * yb
    xo = x_ref[...] + _dot(merged.astype(BF16), wo_ref[...])
    if final:
        ms = jnp.mean(xo * xo, axis=-1, keepdims=True)
        xo = xo * lax.rsqrt(ms + EPS) * fnw_ref[...]
    o_ref[...] = xo


def _out_proj(og, f, szf, sga, sgb, x, wpa, wpb, wo, fnw, final):
    b, t, d = x.shape
    tb = min(ROW_BLOCK, t)
    rows = lambda bi, i: (bi, i, 0)
    const = lambda bi, i: (0, 0)
    row_spec = lambda w: pl.BlockSpec((None, tb, w), rows)
    full = lambda a: pl.BlockSpec(a.shape, const)
    return pl.pallas_call(
        functools.partial(_out_proj_kernel, final=final),
        grid=(b, t // tb),
        in_specs=[row_spec(QK_W), row_spec(F_W), row_spec(F_W), row_spec(d), row_spec(d), row_spec(d),
                  full(wpa), full(wpb), full(wo), full(fnw)],
        out_specs=row_spec(d),
        out_shape=jax.ShapeDtypeStruct((b, t, d), F32),
        compiler_params=pltpu.CompilerParams(
            dimension_semantics=("parallel", "parallel"), vmem_limit_bytes=VMEM_LIMIT),
        name="out_proj",
    )(og, f, szf, sga, sgb, x, wpa, wpb, wo, fnw)


def _pad_lanes(v):
    flat = v.reshape(1, -1).astype(F32)
    return jnp.pad(flat, ((0, 0), (0, LANES - flat.shape[1])))


def _layer_params(norm_w, w_in, conv_w, a_log, dt_bias, o_norm_w, w_pa, w_pb, w_o):
    o_z = 3 * QK_W
    o_ab = o_z + QK_W
    o_uf = o_ab + 2 * N_DIR * N_HEADS
    wqkv = w_in[:, :o_z].astype(BF16)
    wrest = jnp.concatenate([w_in[:, o_z:o_ab], w_in[:, o_uf:]], axis=1).astype(BF16)
    wab = jnp.pad(w_in[:, o_ab:o_uf], ((0, 0), (0, LANES - 2 * N_DIR * N_HEADS)))
    wabh = wab.astype(BF16)
    wabl = (wab - wabh.astype(F32)).astype(BF16)
    convw = jnp.pad(conv_w, ((0, SUBLANES - CONV_K), (0, 0)))
    return dict(norm_w=norm_w.reshape(1, -1), wqkv=wqkv, wrest=wrest, wabh=wabh, wabl=wabl,
                convw=convw, alog=_pad_lanes(a_log), dtb=_pad_lanes(dt_bias),
                onw=o_norm_w.reshape(1, -1), wpa=w_pa.astype(BF16), wpb=w_pb.astype(BF16),
                wo=w_o.astype(BF16))


def _layer(x, p, fnw, final):
    q, k, v, sz, gates, uf, szf, sga, sgb = _in_proj(
        x, p["norm_w"], p["wqkv"], p["wrest"], p["wabh"], p["wabl"], p["convw"], p["alog"], p["dtb"])
    f = _fft_real(uf)
    o_fwd = _delta(q, k, v, gates, reverse=False)
    og = _delta(q, k, v, gates, reverse=True, ofwd=o_fwd, sz=sz, onw=p["onw"])
    return _out_proj(og, f, szf, sga, sgb, x, p["wpa"], p["wpb"], p["wo"], fnw, final)


def _trunk(x, params, fnw):
    depth = len(params)
    for l, p in enumerate(params):
        x = _layer(x, p, fnw, final=(l == depth - 1))
    return x


def kernel(x_prompt, x_sample, norm_w, w_in, conv_w, a_log, dt_bias, o_norm_w, w_pa, w_pb, w_o,
           final_norm_w):
    depth = norm_w.shape[0]
    params = [_layer_params(norm_w[l], w_in[l], conv_w[l], a_log[l], dt_bias[l], o_norm_w[l],
                            w_pa[l], w_pb[l], w_o[l]) for l in range(depth)]
    fnw = final_norm_w.reshape(1, -1)
    return (_trunk(x_prompt, params, fnw), _trunk(x_sample, params, fnw))
```

```python
import functools
import math

import numpy as np
import jax
import jax.numpy as jnp
from jax import lax
from jax.experimental import pallas as pl
from jax.experimental.pallas import tpu as pltpu

D_MODEL = 1024
N_HEADS = 8
HEAD_D = 128
QK_W = N_HEADS * HEAD_D
CONV_K = 5
CONV_PAD = CONV_K // 2
N_DIR = 2
F_GROUPS = 4
F_GROUP_W = 128
F_W = F_GROUPS * F_GROUP_W
EPS = 1e-6

LANES = 128
SUBLANES = 8
MXU_N = 256
DELTA_CHUNK = 128
DELTA_LEVELS = 7
DELTA_STEP_CHUNKS = 4
BF16_TILE_ROWS = 16
ROW_BLOCK = 512
VMEM_LIMIT = 56 * 1024 * 1024

G_LANE = 0
B_LANE = N_DIR * N_HEADS

BF16 = jnp.bfloat16
F32 = jnp.float32


def _dot(a, b):
    return jnp.dot(a, b, preferred_element_type=F32)


def _sigmoid(x):
    return 0.5 + 0.5 * jnp.tanh(0.5 * x)


def _silu(x):
    return x * _sigmoid(x)


def _in_proj_kernel(x_ref, xp_ref, xn_ref, nw_ref, wqkv_ref, wrest_ref, wabh_ref, wabl_ref,
                    convw_ref, alog_ref, dtb_ref,
                    q_ref, k_ref, v_ref, sz_ref, gate_ref, uf_ref, szf_ref, sga_ref, sgb_ref,
                    pext_ref):
    i = pl.program_id(1)
    n = pl.num_programs(1)
    tb = x_ref.shape[0]
    nw = nw_ref[...]

    def norm(xb):
        ms = jnp.mean(xb * xb, axis=-1, keepdims=True)
        return xb * lax.rsqrt(ms + EPS) * nw

    xm = norm(x_ref[...])
    xp = norm(xp_ref[...]) * jnp.where(i > 0, 1.0, 0.0)
    xn = norm(xn_ref[...]) * jnp.where(i < n - 1, 1.0, 0.0)
    xm_bf = xm.astype(BF16)
    xext_bf = jnp.concatenate([xp, xm, xn], axis=0).astype(BF16)

    ones = jnp.ones((HEAD_D, HEAD_D), BF16)
    row0 = SUBLANES - CONV_PAD
    head_cols = [slice(h * HEAD_D, (h + 1) * HEAD_D) for h in range(N_HEADS)]

    def project(seg, hp):
        lo, hi = 2 * hp * HEAD_D, 2 * (hp + 1) * HEAD_D
        pext_ref[seg % 2, :, lo:hi] = _dot(xext_bf, wqkv_ref[:, seg * QK_W + lo:seg * QK_W + hi])

    def conv_head(seg, h, out_ref):
        wcols = slice(seg * QK_W + h * HEAD_D, seg * QK_W + (h + 1) * HEAD_D)
        p = pext_ref[seg % 2, :, head_cols[h]]
        acc = None
        for j in range(CONV_K):
            shift = (SUBLANES - row0 - j) % p.shape[0]
            tap = p if shift == 0 else pltpu.roll(p, shift, axis=0)
            term = convw_ref[j:j + 1, wcols] * tap[SUBLANES:SUBLANES + tb]
            acc = term if acc is None else acc + term
        s = _silu(acc)
        if seg < 2:
            ss = _dot((s * s).astype(BF16), ones)
            s = s * lax.rsqrt(ss + EPS)
            if seg == 0:
                s = s * (HEAD_D ** -0.5)
        out_ref[:, head_cols[h]] = s.astype(out_ref.dtype)

    identity = lambda y: y
    rest = []
    for out_ref, act in ((sz_ref, _silu), (uf_ref, identity), (szf_ref, _silu),
                         (sga_ref, _sigmoid), (sgb_ref, _sigmoid)):
        rest += [(out_ref, act, n) for n in range(out_ref.shape[1] // MXU_N)]

    def rest_piece(n):
        out_ref, act, m = rest[n]
        y = _dot(xm_bf, wrest_ref[:, n * MXU_N:(n + 1) * MXU_N])
        out_ref[:, m * MXU_N:(m + 1) * MXU_N] = act(y).astype(out_ref.dtype)

    pairs = N_HEADS // 2
    for hp in range(pairs):
        project(0, hp)
    rest_pieces = iter(range(len(rest)))
    for seg, out_ref in enumerate((q_ref, k_ref, v_ref)):
        for h in range(N_HEADS):
            if seg < 2 and h < pairs:
                project(seg + 1, h)
            else:
                rest_piece(next(rest_pieces))
            conv_head(seg, h, out_ref)
    for n in rest_pieces:
        rest_piece(n)

    xm_lo = (xm - xm_bf.astype(F32)).astype(BF16)
    wh = wabh_ref[...]
    ab = _dot(xm_bf, wh) + _dot(xm_lo, wh) + _dot(xm_bf, wabl_ref[...])
    lane = lax.broadcasted_iota(jnp.int32, ab.shape, 1)
    xs = ab + dtb_ref[...]
    softplus = jnp.maximum(xs, 0.0) + jnp.log(1.0 + jnp.exp(-jnp.abs(xs)))
    g = jnp.where(lane < B_LANE, -jnp.exp(alog_ref[...]) * softplus, 0.0)
    beta = _sigmoid(ab)

    c = DELTA_CHUNK
    r = lax.broadcasted_iota(jnp.int32, (c, c), 0)
    cidx = lax.broadcasted_iota(jnp.int32, (c, c), 1)
    tril = jnp.where(r >= cidx, 1.0, 0.0).astype(F32)
    triu = jnp.where(r <= cidx, 1.0, 0.0).astype(F32)
    ln = lax.broadcasted_iota(jnp.int32, (c, LANES), 1)
    for ci in range(tb // c):
        rows = slice(ci * c, (ci + 1) * c)
        gch = g[rows]
        cf = jnp.dot(tril, gch, precision=lax.Precision.HIGHEST, preferred_element_type=F32)
        cb = jnp.dot(triu, gch, precision=lax.Precision.HIGHEST, preferred_element_type=F32)
        out = jnp.where(ln < N_HEADS, cf,
                        jnp.where(ln < B_LANE, cb,
                                  jnp.where(ln < 2 * B_LANE, beta[rows], 0.0)))
        gate_ref[rows, :] = out


def _in_proj(x, norm_w, wqkv, wrest, wabh, wabl, convw, alog, dtb):
    b, t, d = x.shape
    tb = min(ROW_BLOCK, t)
    nb = t // tb
    r8 = tb // SUBLANES
    last8 = t // SUBLANES - 1

    def rows(bi, i):
        return (bi, i, 0)

    def const(bi, i):
        return (0, 0)

    row_spec = lambda w: pl.BlockSpec((None, tb, w), rows)
    full = lambda a: pl.BlockSpec(a.shape, const)
    out_shapes = (
        jax.ShapeDtypeStruct((b, t, QK_W), BF16),
        jax.ShapeDtypeStruct((b, t, QK_W), BF16),
        jax.ShapeDtypeStruct((b, t, QK_W), BF16),
        jax.ShapeDtypeStruct((b, t, QK_W), BF16),
        jax.ShapeDtypeStruct((b, t, LANES), F32),
        jax.ShapeDtypeStruct((b, t, F_W), BF16),
        jax.ShapeDtypeStruct((b, t, F_W), BF16),
        jax.ShapeDtypeStruct((b, t, D_MODEL), BF16),
        jax.ShapeDtypeStruct((b, t, D_MODEL), BF16),
    )
    return pl.pallas_call(
        _in_proj_kernel,
        grid=(b, nb),
        in_specs=[
            row_spec(d),
            pl.BlockSpec((None, SUBLANES, d), lambda bi, i: (bi, jnp.maximum(i * r8 - 1, 0), 0)),
            pl.BlockSpec((None, SUBLANES, d), lambda bi, i: (bi, jnp.minimum((i + 1) * r8, last8), 0)),
            full(norm_w), full(wqkv), full(wrest), full(wabh), full(wabl), full(convw),
            full(alog), full(dtb),
        ],
        out_specs=[row_spec(s.shape[-1]) for s in out_shapes],
        out_shape=out_shapes,
        scratch_shapes=[pltpu.VMEM((2, tb + 2 * SUBLANES, QK_W), F32)],
        compiler_params=pltpu.CompilerParams(
            dimension_semantics=("parallel", "parallel"), vmem_limit_bytes=VMEM_LIMIT),
        name="in_proj",
    )(x, x, x, norm_w, wqkv, wrest, wabh, wabl, convw, alog, dtb)


@functools.lru_cache(maxsize=None)
def _fft_tables(n1, n2):
    t = n1 * n2
    c = np.arange(F_GROUP_W)
    ang = 2.0 * np.pi * ((c[:, None] * c[None, :]) % F_GROUP_W) / F_GROUP_W
    sc = 1.0 / math.sqrt(F_GROUP_W)
    wc = np.concatenate([np.cos(ang), -np.sin(ang)], axis=1) * sc
    a = np.arange(n1)
    ang1 = 2.0 * np.pi * ((a[:, None] * a[None, :]) % n1) / n1
    fr, fi = np.cos(ang1) / math.sqrt(n1), -np.sin(ang1) / math.sqrt(n1)
    m1 = np.block([[fr, -fi], [fi, fr]])
    t1p = np.arange(n1)[:, None, None]
    t2p = np.arange(n2)[None, :, None]
    t2 = np.arange(n2)[None, None, :]
    ang2 = 2.0 * np.pi * ((t2 * (t1p + n1 * t2p)) % t) / t
    g = np.concatenate([np.cos(ang2), np.sin(ang2)], axis=2) / math.sqrt(n2)
    as_bf = lambda m: m.astype(np.float32).astype(BF16)
    return as_bf(wc), as_bf(m1), as_bf(g)


def _fft_a_kernel(u_ref, wc_ref, m1_ref, yr_ref, yi_ref):
    n1 = u_ref.shape[0]
    wc = wc_ref[...]
    m1 = m1_ref[...]
    for tt in range(yr_ref.shape[1]):
        ub = u_ref[:, tt * F_W:(tt + 1) * F_W]
        zr, zi = [], []
        for gi in range(F_GROUPS):
            z = _dot(ub[:, gi * F_GROUP_W:(gi + 1) * F_GROUP_W], wc)
            zr.append(z[:, :F_GROUP_W])
            zi.append(z[:, F_GROUP_W:])
        zst = jnp.concatenate([jnp.concatenate(zr, axis=1), jnp.concatenate(zi, axis=1)], axis=0)
        y = _dot(m1, zst.astype(BF16))
        yr_ref[:, tt, :] = y[:n1]
        yi_ref[:, tt, :] = y[n1:]


def _fft_b_kernel(yr_ref, yi_ref, g_ref, f_ref):
    n2 = g_ref.shape[1]
    for j in range(g_ref.shape[0]):
        rows = slice(j * n2, (j + 1) * n2)
        yst = jnp.concatenate([yr_ref[rows, :], yi_ref[rows, :]], axis=0).astype(BF16)
        f_ref[:, j, :] = _dot(g_ref[j], yst)


def _fft_real(uf):
    b, t, _ = uf.shape
    n1 = 1 << ((t.bit_length() - 1) // 2)
    n2 = t // n1
    wc, m1, g = _fft_tables(n1, n2)
    u2 = uf.reshape(b, n1, n2 * F_W)
    blk_a = pl.BlockSpec((None, n1, SUBLANES, F_W), lambda bi, i: (bi, 0, i, 0))
    yr, yi = pl.pallas_call(
        _fft_a_kernel,
        grid=(b, n2 // SUBLANES),
        in_specs=[pl.BlockSpec((None, n1, SUBLANES * F_W), lambda bi, i: (bi, 0, i)),
                  pl.BlockSpec(wc.shape, lambda bi, i: (0, 0)),
                  pl.BlockSpec(m1.shape, lambda bi, i: (0, 0))],
        out_specs=[blk_a, blk_a],
        out_shape=(jax.ShapeDtypeStruct((b, n1, n2, F_W), F32),) * 2,
        compiler_params=pltpu.CompilerParams(dimension_semantics=("parallel", "parallel")),
        name="fft_a",
    )(u2, wc, m1)
    yr = yr.reshape(b, t, F_W)
    yi = yi.reshape(b, t, F_W)
    blk_y = pl.BlockSpec((None, SUBLANES * n2, F_W), lambda bi, i: (bi, i, 0))
    f = pl.pallas_call(
        _fft_b_kernel,
        grid=(b, n1 // SUBLANES),
        in_specs=[blk_y, blk_y,
                  pl.BlockSpec((SUBLANES, n2, 2 * n2), lambda bi, i: (i, 0, 0))],
        out_specs=pl.BlockSpec((None, n2, SUBLANES, F_W), lambda bi, i: (bi, 0, i, 0)),
        out_shape=jax.ShapeDtypeStruct((b, n2, n1, F_W), F32),
        compiler_params=pltpu.CompilerParams(dimension_semantics=("parallel", "parallel")),
        name="fft_b",
    )(yr, yi, g)
    return f.reshape(b, t, F_W)


def _delta_kernel(*refs, reverse, add_first):
    if add_first:
        q_ref, k_ref, v_ref, gate_ref, ofwd_ref, o_ref, s_ref = refs
    else:
        q_ref, k_ref, v_ref, gate_ref, o_ref, s_ref = refs
    c = DELTA_CHUNK

    @pl.when(pl.program_id(1) == 0)
    def _():
        s_ref[...] = jnp.zeros_like(s_ref)

    nch = q_ref.shape[0] // c
    row = lax.broadcasted_iota(jnp.int32, (c, c), 0)
    col = lax.broadcasted_iota(jnp.int32, (c, c), 1)
    if reverse:
        valid, strict, last = row <= col, row < col, 0
    else:
        valid, strict, last = row >= col, row > col, c - 1
    eye =jnp.where(row == col, 1.0, 0.0).astype(F32)
    level_masks = [((row >> l) ^ (col >> l)) == 1 for l in range(DELTA_LEVELS)]
    nt_dims = (((1,), (1,)), ((), ()))
    tn_dims = (((0,), (0,)), ((), ()))

    heads = range(N_HEADS)
    rows = [slice(ci * c, (ci + 1) * c) for ci in range(nch)]
    cols = [slice(h * HEAD_D, (h + 1) * HEAD_D) for h in heads]
    dir_off = N_HEADS if reverse else 0
    g_lane = [G_LANE + dir_off + h for h in heads]
    b_lane = [B_LANE + dir_off + h for h in heads]
    gates, lmat, attn, x, uw = {}, {}, {}, {}, {}

    def lane_bcast(m, lane):
        return jnp.broadcast_to(m[:, lane:lane + 1], (c, c))

    def prep_stage(ci):
        gcol = gate_ref[rows[ci], :]
        g_end = gcol[last:last + 1, :]
        gates[ci] = dict(col=gcol, row=gcol.T, e=jnp.exp(gcol), e_end=jnp.exp(g_end),
                         e_rest=jnp.exp(jnp.minimum(g_end - gcol, 0.0)))
        yield
        for h in heads:
            gt = gates[ci]
            g_i = lane_bcast(gt["col"], g_lane[h])
            g_j = jnp.broadcast_to(gt["row"][g_lane[h]:g_lane[h] + 1, :], (c, c))
            decay = jnp.exp(jnp.where(valid, g_i - g_j, -1e30))
            k = k_ref[rows[ci], cols[h]]
            qk = jnp.concatenate([q_ref[rows[ci], cols[h]], k], axis=0)
            gram = lax.dot_general(qk, k, nt_dims, preferred_element_type=F32)
            attn[ci, h] = (gram[:c] * decay).astype(BF16)
            lmat[ci, h] = jnp.where(strict, lane_bcast(gt["col"], b_lane[h]) * gram[c:] * decay, 0.0)
            yield

    def take_rows(m, blocks, active):
        return jnp.concatenate([m[blk] for blk, a in zip(blocks, active) if a], axis=0)

    def put_rows(part, rest, blocks, active):
        out, n = [], 0
        for blk, a in zip(blocks, active):
            size = blk.stop - blk.start
            if a:
                piece = part[n * size:(n + 1) * size]
                out.append(piece if rest is None else rest[blk] - piece)
                n += 1
            else:
                out.append(jnp.zeros((size, c), part.dtype) if rest is None else rest[blk])
        return jnp.concatenate(out, axis=0)

    def inverse_stage(ci):
        for h in heads:
            x[ci, h] = eye - jnp.where(level_masks[0], lmat[ci, h], 0.0)
        yield
        for l in range(1, DELTA_LEVELS):
            size = 1 << l
            full = size < BF16_TILE_ROWS
            blocks = [slice(bi * size, (bi + 1) * size) for bi in range(c // size)]
            active = [full or (bi % 2 == (0 if reverse else 1)) for bi in range(c // size)]
            xb, cx = {}, {}
            for h in heads:
                xb[h] = x[ci, h].astype(BF16)
                cl = take_rows(jnp.where(level_masks[l], lmat[ci, h], 0.0), blocks, active)
                cx[h] = put_rows(_dot(cl.astype(BF16), xb[h]).astype(BF16), None, blocks, active)
            yield
            for h in heads:
                upd = _dot(take_rows(x[ci, h], blocks, active).astype(BF16), cx[h])
                x[ci, h] = put_rows(upd, x[ci, h], blocks, active)
            yield
        gt = gates[ci]
        for h in heads:
            b_j = jnp.broadcast_to(gt["row"][b_lane[h]:b_lane[h] + 1, :], (c, c))
            e_i = lane_bcast(gt["e"], g_lane[h])
            kg = (k_ref[rows[ci], cols[h]].astype(F32) * e_i).astype(BF16)
            tb_ = (x[ci, h] * b_j).astype(BF16)
            neg_w = (-_dot(tb_, kg)).astype(BF16)
            q_dec = (q_ref[rows[ci], cols[h]].astype(F32) * e_i).astype(BF16)
            uw[ci, h] = (jnp.concatenate([tb_, neg_w], axis=1),
                         jnp.concatenate([q_dec, attn[ci, h]], axis=1))
        yield

    def scan_stage(ci):
        gt = gates[ci]
        group = 2
        s_bf, v_new = {}, {}
        for h0 in range(0, N_HEADS, group):
            for h in range(h0, h0 + group):
                s_bf[h] = s_ref[h].astype(BF16)
                vs = jnp.concatenate([v_ref[rows[ci], cols[h]], s_bf[h]], axis=0)
                v_new[h] = _dot(uw[ci, h][0], vs).astype(BF16)
            yield
        for h0 in range(0, N_HEADS, group):
            for h in range(h0, h0 + group):
                k_dec = (k_ref[rows[ci], cols[h]].astype(F32)
                         * lane_bcast(gt["e_rest"], g_lane[h])).astype(BF16)
                e_end = jnp.broadcast_to(gt["e_end"][:, g_lane[h]:g_lane[h] + 1], (c, c))
                s_ref[h] = s_ref[h] * e_end + lax.dot_general(
                    k_dec, v_new[h], tn_dims, preferred_element_type=F32)
            yield
        for h0 in range(0, N_HEADS, group):
            for h in range(h0, h0 + group):
                o = _dot(uw[ci, h][1], jnp.concatenate([s_bf[h], v_new[h]], axis=0))
                if add_first:
                    o = o + ofwd_ref[rows[ci], cols[h]]
                o_ref[rows[ci], cols[h]] = o
            yield

    order = list(range(nch - 1, -1, -1) if reverse else range(nch))
    stages = (prep_stage, inverse_stage, scan_stage)
    done = object()
    for tick in range(nch + len(stages) - 1):
        running = [stage(order[tick - d]) for d, stage in enumerate(stages) if 0 <= tick - d < nch]
        while running:
            running = [g for g in running if next(g, done) is not done]


def _delta(q, k, v, gates, reverse, ofwd=None):
    b, t, _ = q.shape
    c = DELTA_CHUNK * min(DELTA_STEP_CHUNKS, t // DELTA_CHUNK)
    nc = t // c
    add_first = ofwd is not None
    if reverse:
        rows = lambda bi, i: (bi, nc - 1 - i, 0)
    else:
        rows = lambda bi, i: (bi, i, 0)
    wide = pl.BlockSpec((None, c, QK_W), rows)
    in_specs = [wide, wide, wide, pl.BlockSpec((None, c, LANES), rows)]
    args = [q, k, v, gates]
    if add_first:
        in_specs += [wide]
        args += [ofwd]
    return pl.pallas_call(
        functools.partial(_delta_kernel, reverse=reverse, add_first=add_first),
        grid=(b, nc),
        in_specs=in_specs,
        out_specs=wide,
        out_shape=jax.ShapeDtypeStruct((b, t, QK_W), F32),
        scratch_shapes=[pltpu.VMEM((N_HEADS, HEAD_D, HEAD_D), F32)],
        compiler_params=pltpu.CompilerParams(dimension_semantics=("arbitrary", "arbitrary")),
        name="delta_bwd" if reverse else "delta_fwd",
    )(*args)


def _out_proj_kernel(o_ref_in, sz_ref, onw_ref, f_ref, szf_ref, sga_ref, sgb_ref, x_ref,
                     wpa_ref, wpb_ref, wo_ref, fnw_ref, o_ref, og_ref, *, final):
    ones = jnp.ones((HEAD_D, HEAD_D), BF16)
    onw = onw_ref[...]
    for h in range(N_HEADS):
        cols = slice(h * HEAD_D, (h + 1) * HEAD_D)
        oh = o_ref_in[:, cols]
        ms = _dot((oh * oh).astype(BF16), ones) * (1.0 / HEAD_D)
        og_ref[:, cols] = (oh * lax.rsqrt(ms + EPS) * onw * sz_ref[:, cols].astype(F32)).astype(BF16)
    ya = _dot(og_ref[...], wpa_ref[...])
    fg =(f_ref[...] * szf_ref[...].astype(F32)).astype(BF16)
    yb = _dot(fg, wpb_ref[...])
    merged = sga_ref[...].astype(F32) * ya + sgb_ref[...].astype(F32) * yb
    xo = x_ref[...] + _dot(merged.astype(BF16), wo_ref[...])
    if final:
        ms = jnp.mean(xo * xo, axis=-1, keepdims=True)
        xo = xo * lax.rsqrt(ms + EPS) * fnw_ref[...]
    o_ref[...] = xo


def _out_proj(o, sz, onw, f, szf, sga, sgb, x, wpa, wpb, wo, fnw, final):
    b, t, d = x.shape
    tb = min(ROW_BLOCK, t)
    rows = lambda bi, i: (bi, i, 0)
    const = lambda bi, i: (0, 0)
    row_spec = lambda w: pl.BlockSpec((None, tb, w), rows)
    full = lambda a: pl.BlockSpec(a.shape, const)
    return pl.pallas_call(
        functools.partial(_out_proj_kernel, final=final),
        grid=(b, t // tb),
        in_specs=[row_spec(QK_W), row_spec(QK_W), full(onw), row_spec(F_W), row_spec(F_W),
                  row_spec(d), row_spec(d), row_spec(d),
                  full(wpa), full(wpb), full(wo), full(fnw)],
        out_specs=row_spec(d),
        out_shape=jax.ShapeDtypeStruct((b, t, d), F32),
        scratch_shapes=[pltpu.VMEM((tb, QK_W), BF16)],
        compiler_params=pltpu.CompilerParams(
            dimension_semantics=("parallel", "parallel"), vmem_limit_bytes=VMEM_LIMIT),
        name="out_proj",
    )(o, sz, onw, f, szf, sga, sgb, x, wpa, wpb, wo, fnw)


def _pad_lanes(v):
    flat = v.reshape(1, -1).astype(F32)
    return jnp.pad(flat, ((0, 0), (0, LANES - flat.shape[1])))


def _layer_params(norm_w, w_in, conv_w, a_log, dt_bias, o_norm_w, w_pa, w_pb, w_o):
    o_z = 3 * QK_W
    o_ab = o_z + QK_W
    o_uf = o_ab + 2 * N_DIR * N_HEADS
    wqkv = w_in[:, :o_z].astype(BF16)
    wrest = jnp.concatenate([w_in[:, o_z:o_ab], w_in[:, o_uf:]], axis=1).astype(BF16)
    wab = jnp.pad(w_in[:, o_ab:o_uf], ((0, 0), (0, LANES - 2 * N_DIR * N_HEADS)))
    wabh = wab.astype(BF16)
    wabl = (wab - wabh.astype(F32)).astype(BF16)
    convw = jnp.pad(conv_w, ((0, SUBLANES - CONV_K), (0, 0)))
    return dict(norm_w=norm_w.reshape(1, -1), wqkv=wqkv, wrest=wrest, wabh=wabh, wabl=wabl,
                convw=convw, alog=_pad_lanes(a_log), dtb=_pad_lanes(dt_bias),
                onw=o_norm_w.reshape(1, -1), wpa=w_pa.astype(BF16), wpb=w_pb.astype(BF16),
                wo=w_o.astype(BF16))


def _layer(x, p, fnw, final):
    q, k, v, sz, gates, uf, szf, sga, sgb = _in_proj(
        x, p["norm_w"], p["wqkv"], p["wrest"], p["wabh"], p["wabl"], p["convw"], p["alog"], p["dtb"])
    f = _fft_real(uf)
    o_fwd = _delta(q, k, v, gates, reverse=False)
    o = _delta(q, k, v, gates, reverse=True, ofwd=o_fwd)
    return _out_proj(o, sz, p["onw"], f, szf, sga, sgb, x, p["wpa"], p["wpb"], p["wo"], fnw, final)


def _trunk(x, params, fnw):
    depth = len(params)
    for l, p in enumerate(params):
        x = _layer(x, p, fnw, final=(l == depth - 1))
    return x


def kernel(x_prompt, x_sample, norm_w, w_in, conv_w, a_log, dt_bias, o_norm_w, w_pa, w_pb, w_o,
           final_norm_w):
    depth = norm_w.shape[0]
    params = [_layer_params(norm_w[l], w_in[l], conv_w[l], a_log[l], dt_bias[l], o_norm_w[l],
                            w_pa[l], w_pb[l], w_o[l]) for l in range(depth)]
    fnw = final_norm_w.reshape(1, -1)
    return (_trunk(x_prompt, params, fnw), _trunk(x_sample, params, fnw))
```

```python
import functools
import math

import numpy as np
import jax
import jax.numpy as jnp
from jax import lax
from jax.experimental import pallas as pl
from jax.experimental.pallas import tpu as pltpu

D_MODEL = 1024
N_HEADS = 8
HEAD_D = 128
QK_W = N_HEADS * HEAD_D
CONV_K = 5
CONV_PAD = CONV_K // 2
N_DIR = 2
F_GROUPS = 4
F_GROUP_W = 128
F_W = F_GROUPS * F_GROUP_W
EPS = 1e-6

LANES = 128
SUBLANES = 8
MXU_N = 256
DELTA_CHUNK = 128
DELTA_LEVELS = 7
DELTA_STEP_CHUNKS = 4
DELTA_GROUP_CHUNKS = 2
BF16_TILE_ROWS = 16
ROW_BLOCK = 512
VMEM_LIMIT = 56 * 1024 * 1024

G_LANE = 0
B_LANE = N_DIR * N_HEADS

BF16 = jnp.bfloat16
F32 = jnp.float32


def _dot(a, b):
    return jnp.dot(a, b, preferred_element_type=F32)


def _sigmoid(x):
    return 0.5 + 0.5 * jnp.tanh(0.5 * x)


def _silu(x):
    return x * _sigmoid(x)


def _in_proj_kernel(x_ref, xp_ref, xn_ref, nw_ref, wqkv_ref, wrest_ref, wab_ref,
                    convw_ref, alog_ref, dtb_ref,
                    q_ref, k_ref, v_ref, sz_ref, gate_ref, uf_ref, szf_ref, sga_ref, sgb_ref,
                    pext_ref):
    i = pl.program_id(1)
    n = pl.num_programs(1)
    tb = x_ref.shape[0]
    nw = nw_ref[...]

    def norm(xb):
        ms = jnp.mean(xb * xb, axis=-1, keepdims=True)
        return xb * lax.rsqrt(ms + EPS) * nw

    xm = norm(x_ref[...])
    xp = norm(xp_ref[...]) * jnp.where(i > 0, 1.0, 0.0)
    xn = norm(xn_ref[...]) * jnp.where(i < n - 1, 1.0, 0.0)
    xm_bf = xm.astype(BF16)
    xext_bf = jnp.concatenate([xp, xm, xn], axis=0).astype(BF16)

    ones = jnp.ones((HEAD_D, HEAD_D), BF16)
    row0 = SUBLANES - CONV_PAD
    head_cols = [slice(h * HEAD_D, (h + 1) * HEAD_D) for h in range(N_HEADS)]

    def project(seg, hp):
        lo, hi = 2 * hp * HEAD_D, 2 * (hp + 1) * HEAD_D
        pext_ref[seg % 2, :, lo:hi] = _dot(xext_bf, wqkv_ref[:, seg * QK_W + lo:seg * QK_W + hi])

    def conv_head(seg, h, out_ref):
        wcols = slice(seg * QK_W + h * HEAD_D, seg * QK_W + (h + 1) * HEAD_D)
        p = pext_ref[seg % 2, :, head_cols[h]]
        acc = None
        for j in range(CONV_K):
            shift = (SUBLANES - row0 - j) % p.shape[0]
            tap = p if shift == 0 else pltpu.roll(p, shift, axis=0)
            term = convw_ref[j:j + 1, wcols] * tap[SUBLANES:SUBLANES + tb]
            acc = term if acc is None else acc + term
        s = _silu(acc)
        if seg < 2:
            ss = _dot((s * s).astype(BF16), ones)
            inv = lax.rsqrt(ss + EPS)
            s = s * (inv * (HEAD_D ** -0.5) if seg == 0 else inv)
        out_ref[:, head_cols[h]] = s.astype(out_ref.dtype)

    identity = lambda y: y
    rest = []
    for out_ref, act in ((sz_ref, _silu), (uf_ref, identity), (szf_ref, _silu),
                         (sga_ref, _sigmoid), (sgb_ref, _sigmoid)):
        rest += [(out_ref, act, n) for n in range(out_ref.shape[1] // MXU_N)]

    def rest_piece(n):
        out_ref, act, m = rest[n]
        y = _dot(xm_bf, wrest_ref[:, n * MXU_N:(n + 1) * MXU_N])
        out_ref[:, m * MXU_N:(m + 1) * MXU_N] = act(y).astype(out_ref.dtype)

    pairs = N_HEADS // 2
    for hp in range(pairs):
        project(0, hp)
    rest_pieces = iter(range(len(rest)))
    for seg, out_ref in enumerate((q_ref, k_ref, v_ref)):
        for h in range(N_HEADS):
            if seg < 2 and h < pairs:
                project(seg + 1, h)
            else:
                rest_piece(next(rest_pieces))
            conv_head(seg, h, out_ref)
    for n in rest_pieces:
        rest_piece(n)

    xm_lo = (xm - xm_bf.astype(F32)).astype(BF16)
    hi = _dot(xm_bf, wab_ref[...])
    ab = hi[:, :LANES] + hi[:, LANES:] + _dot(xm_lo, wab_ref[:, :LANES])
    lane = lax.broadcasted_iota(jnp.int32, ab.shape, 1)
    xs = ab + dtb_ref[...]
    softplus = jnp.maximum(xs, 0.0) + jnp.log(1.0 + jnp.exp(-jnp.abs(xs)))
    g = jnp.where(lane < B_LANE, -jnp.exp(alog_ref[...]) * softplus, 0.0)
    beta = _sigmoid(ab)

    c = DELTA_CHUNK
    r = lax.broadcasted_iota(jnp.int32, (c, c), 0)
    cidx = lax.broadcasted_iota(jnp.int32, (c, c), 1)
    tril = jnp.where(r >= cidx, 1.0, 0.0).astype(F32)
    ln = lax.broadcasted_iota(jnp.int32, (c, LANES), 1)
    for ci in range(tb // c):
        rows = slice(ci * c, (ci + 1) * c)
        gch = g[rows]
        cf = jnp.dot(tril, gch, precision=lax.Precision.HIGHEST, preferred_element_type=F32)
        cb = cf[c - 1:c, :] - cf + gch
        out = jnp.where(ln < N_HEADS, cf,
                        jnp.where(ln < B_LANE, cb,
                                  jnp.where(ln < 2 * B_LANE, beta[rows], 0.0)))
        gate_ref[rows, :] = out


def _in_proj(x, norm_w, wqkv, wrest, wab, convw, alog, dtb):
    b, t, d = x.shape
    tb = min(ROW_BLOCK, t)
    nb = t // tb
    r8 = tb // SUBLANES
    last8 = t // SUBLANES - 1

    def rows(bi, i):
        return (bi, i, 0)

    def const(bi, i):
        return (0, 0)

    row_spec = lambda w: pl.BlockSpec((None, tb, w), rows)
    full = lambda a: pl.BlockSpec(a.shape, const)
    out_shapes = (
        jax.ShapeDtypeStruct((b, t, QK_W), BF16),
        jax.ShapeDtypeStruct((b, t, QK_W), BF16),
        jax.ShapeDtypeStruct((b, t, QK_W), BF16),
        jax.ShapeDtypeStruct((b, t, QK_W), BF16),
        jax.ShapeDtypeStruct((b, t, LANES), F32),
        jax.ShapeDtypeStruct((b, t, F_W), BF16),
        jax.ShapeDtypeStruct((b, t, F_W), BF16),
        jax.ShapeDtypeStruct((b, t, D_MODEL), BF16),
        jax.ShapeDtypeStruct((b, t, D_MODEL), BF16),
    )
    return pl.pallas_call(
        _in_proj_kernel,
        grid=(b, nb),
        in_specs=[
            row_spec(d),
            pl.BlockSpec((None, SUBLANES, d), lambda bi, i: (bi, jnp.maximum(i * r8 - 1, 0), 0)),
            pl.BlockSpec((None, SUBLANES, d), lambda bi, i: (bi, jnp.minimum((i + 1) * r8, last8), 0)),
            full(norm_w), full(wqkv), full(wrest), full(wab), full(convw),
            full(alog), full(dtb),
        ],
        out_specs=[row_spec(s.shape[-1]) for s in out_shapes],
        out_shape=out_shapes,
        scratch_shapes=[pltpu.VMEM((2, tb + 2 * SUBLANES, QK_W), F32)],
        compiler_params=pltpu.CompilerParams(
            dimension_semantics=("parallel", "parallel"), vmem_limit_bytes=VMEM_LIMIT),
        name="in_proj",
    )(x, x, x, norm_w, wqkv, wrest, wab, convw, alog, dtb)


@functools.lru_cache(maxsize=None)
def _fft_tables(n1, n2):
    t = n1 * n2
    c = np.arange(F_GROUP_W)
    ang = 2.0 * np.pi * ((c[:, None] * c[None, :]) % F_GROUP_W) / F_GROUP_W
    sc = 1.0 / math.sqrt(F_GROUP_W)
    wc = np.concatenate([np.cos(ang), -np.sin(ang)], axis=1) * sc
    a = np.arange(n1)
    ang1 = 2.0 * np.pi * ((a[:, None] * a[None, :]) % n1) / n1
    fr, fi = np.cos(ang1) / math.sqrt(n1), -np.sin(ang1) / math.sqrt(n1)
    m1 = np.block([[fr, -fi], [fi, fr]])
    t1p = np.arange(n1)[:, None, None]
    t2p = np.arange(n2)[None, :, None]
    t2 = np.arange(n2)[None, None, :]
    ang2 = 2.0 * np.pi * ((t2 * (t1p + n1 * t2p)) % t) / t
    g = np.concatenate([np.cos(ang2), np.sin(ang2)], axis=2) / math.sqrt(n2)
    as_bf = lambda m: m.astype(np.float32).astype(BF16)
    return as_bf(wc), as_bf(m1), as_bf(g)


def _fft_a_kernel(u_ref, wc_ref, m1_ref, yr_ref, yi_ref):
    n1 = u_ref.shape[0]
    wc = wc_ref[...]
    m1 = m1_ref[...]
    for tt in range(yr_ref.shape[1]):
        ub = u_ref[:, tt * F_W:(tt + 1) * F_W]
        zr, zi = [], []
        for gi in range(F_GROUPS):
            z = _dot(ub[:, gi * F_GROUP_W:(gi + 1) * F_GROUP_W], wc)
            zr.append(z[:, :F_GROUP_W])
            zi.append(z[:, F_GROUP_W:])
        zst = jnp.concatenate([jnp.concatenate(zr, axis=1), jnp.concatenate(zi, axis=1)], axis=0)
        y = _dot(m1, zst.astype(BF16))
        yr_ref[:, tt, :] = y[:n1]
        yi_ref[:, tt, :] = y[n1:]


def _fft_b_kernel(yr_ref, yi_ref, g_ref, f_ref):
    n2 = g_ref.shape[1]
    for j in range(g_ref.shape[0]):
        rows = slice(j * n2, (j + 1) * n2)
        yst = jnp.concatenate([yr_ref[rows, :], yi_ref[rows, :]], axis=0).astype(BF16)
        f_ref[:, j, :] = _dot(g_ref[j], yst)


def _fft_real(uf):
    b, t, _ = uf.shape
    n1 = 1 << ((t.bit_length() - 1) // 2)
    n2 = t // n1
    wc, m1, g = _fft_tables(n1, n2)
    u2 = uf.reshape(b, n1, n2 * F_W)
    blk_a = pl.BlockSpec((None, n1, SUBLANES, F_W), lambda bi, i: (bi, 0, i, 0))
    yr, yi = pl.pallas_call(
        _fft_a_kernel,
        grid=(b, n2 // SUBLANES),
        in_specs=[pl.BlockSpec((None, n1, SUBLANES * F_W), lambda bi, i: (bi, 0, i)),
                  pl.BlockSpec(wc.shape, lambda bi, i: (0, 0)),
                  pl.BlockSpec(m1.shape, lambda bi, i: (0, 0))],
        out_specs=[blk_a, blk_a],
        out_shape=(jax.ShapeDtypeStruct((b, n1, n2, F_W), F32),) * 2,
        compiler_params=pltpu.CompilerParams(dimension_semantics=("parallel", "parallel")),
        name="fft_a",
    )(u2, wc, m1)
    yr = yr.reshape(b, t, F_W)
    yi = yi.reshape(b, t, F_W)
    blk_y = pl.BlockSpec((None, SUBLANES * n2, F_W), lambda bi, i: (bi, i, 0))
    f = pl.pallas_call(
        _fft_b_kernel,
        grid=(b, n1 // SUBLANES),
        in_specs=[blk_y, blk_y,
                  pl.BlockSpec((SUBLANES, n2, 2 * n2), lambda bi, i: (i, 0, 0))],
        out_specs=pl.BlockSpec((None, n2, SUBLANES, F_W), lambda bi, i: (bi, 0, i, 0)),
        out_shape=jax.ShapeDtypeStruct((b, n2, n1, F_W), F32),
        compiler_params=pltpu.CompilerParams(dimension_semantics=("parallel", "parallel")),
        name="fft_b",
    )(yr, yi, g)
    return f.reshape(b, t, F_W)


def _delta_kernel(*refs, reverse, add_first):
    if add_first:
        q_ref, k_ref, v_ref, gate_ref, ofwd_ref, o_ref, s_ref = refs
    else:
        q_ref, k_ref, v_ref, gate_ref, o_ref, s_ref = refs
    c = DELTA_CHUNK

    @pl.when(pl.program_id(1) == 0)
    def _():
        s_ref[...] = jnp.zeros_like(s_ref)

    nch = q_ref.shape[0] // c
    row = lax.broadcasted_iota(jnp.int32, (c, c), 0)
    col = lax.broadcasted_iota(jnp.int32, (c, c), 1)
    if reverse:
        valid, strict, last = row <= col, row < col, 0
    else:
        valid, strict, last = row >= col, row > col, c - 1
    eye =jnp.where(row == col, 1.0, 0.0).astype(F32)
    level_masks = [((row >> l) ^ (col >> l)) == 1 for l in range(DELTA_LEVELS)]
    nt_dims = (((1,), (1,)), ((), ()))
    tn_dims = (((0,), (0,)), ((), ()))

    heads = range(N_HEADS)
    rows = [slice(ci * c, (ci + 1) * c) for ci in range(nch)]
    cols = [slice(h * HEAD_D, (h + 1) * HEAD_D) for h in heads]
    dir_off = N_HEADS if reverse else 0
    g_lane = [G_LANE + dir_off + h for h in heads]
    b_lane = [B_LANE + dir_off + h for h in heads]
    gates, lmat, attn, x, uw = {}, {}, {}, {}, {}

    def lane_bcast(m, lane):
        return jnp.broadcast_to(m[:, lane:lane + 1], (c, c))

    def prep_stage(group):
        for ci in group:
            gcol = gate_ref[rows[ci], :]
            g_end = gcol[last:last + 1, :]
            gt = dict(col=gcol, row=gcol.T, e=jnp.exp(gcol), e_end=jnp.exp(g_end),
                      e_rest=jnp.exp(jnp.minimum(g_end - gcol, 0.0)))
            gates[ci] = gt
            yield
            for h in heads:
                g_i = lane_bcast(gt["col"], g_lane[h])
                g_j = jnp.broadcast_to(gt["row"][g_lane[h]:g_lane[h] + 1, :], (c, c))
                decay = jnp.exp(jnp.where(valid, g_i - g_j, -1e30))
                k = k_ref[rows[ci], cols[h]]
                qk = jnp.concatenate([q_ref[rows[ci], cols[h]], k], axis=0)
                gram = lax.dot_general(qk, k, nt_dims, preferred_element_type=F32)
                attn[ci, h] = (gram[:c] * decay).astype(BF16)
                lmat[ci, h] = jnp.where(
                    strict, lane_bcast(gt["col"], b_lane[h]) * gram[c:] * decay, 0.0)
                yield

    def take_rows(m, blocks, active):
        return jnp.concatenate([m[blk] for blk, a in zip(blocks, active) if a], axis=0)

    def put_rows(part, rest, blocks, active):
        out, n = [], 0
        for blk, a in zip(blocks, active):
            size = blk.stop - blk.start
            if a:
                piece = part[n * size:(n + 1) * size]
                out.append(piece if rest is None else rest[blk] - piece)
                n += 1
            else:
                out.append(jnp.zeros((size, c), part.dtype) if rest is None else rest[blk])
        return jnp.concatenate(out, axis=0)

    def inverse_stage(group):
        pairs = [(ci, h) for ci in group for h in heads]
        for p in pairs:
            x[p] = eye - jnp.where(level_masks[0], lmat[p], 0.0)
        yield
        for l in range(1, DELTA_LEVELS):
            size = 1 << l
            full = size < BF16_TILE_ROWS
            blocks = [slice(bi * size, (bi + 1) * size) for bi in range(c // size)]
            active = [full or (bi % 2 == (0 if reverse else 1)) for bi in range(c // size)]
            cx = {}
            for p in pairs:
                cl = take_rows(jnp.where(level_masks[l], lmat[p], 0.0), blocks, active)
                cx[p] = put_rows(_dot(cl.astype(BF16), x[p].astype(BF16)).astype(BF16), None,
                                 blocks, active)
            yield
            for p in pairs:
                upd = _dot(take_rows(x[p], blocks, active).astype(BF16), cx[p])
                x[p] = put_rows(upd, x[p], blocks, active)
            yield
        for ci, h in pairs:
            gt = gates[ci]
            b_j = jnp.broadcast_to(gt["row"][b_lane[h]:b_lane[h] + 1, :], (c, c))
            e_i = lane_bcast(gt["e"], g_lane[h])
            kg = (k_ref[rows[ci], cols[h]].astype(F32) * e_i).astype(BF16)
            tb_ = (x[ci, h] * b_j).astype(BF16)
            neg_w = (-_dot(tb_, kg)).astype(BF16)
            q_dec = (q_ref[rows[ci], cols[h]].astype(F32) * e_i).astype(BF16)
            uw[ci, h] = (jnp.concatenate([tb_, neg_w], axis=1),
                         jnp.concatenate([q_dec, attn[ci, h]], axis=1))
        yield

    def scan_stage(group):
        piece = 2
        for ci in group:
            gt = gates[ci]
            s_bf, v_new = {}, {}
            for h0 in range(0, N_HEADS, piece):
                for h in range(h0, h0 + piece):
                    s_bf[h] = s_ref[h].astype(BF16)
                    vs = jnp.concatenate([v_ref[rows[ci], cols[h]], s_bf[h]], axis=0)
                    v_new[h] = _dot(uw[ci, h][0], vs).astype(BF16)
                yield
            for h0 in range(0, N_HEADS, piece):
                for h in range(h0, h0 + piece):
                    k_dec = (k_ref[rows[ci], cols[h]].astype(F32)
                             * lane_bcast(gt["e_rest"], g_lane[h])).astype(BF16)
                    e_end = jnp.broadcast_to(gt["e_end"][:, g_lane[h]:g_lane[h] + 1], (c, c))
                    s_ref[h] = s_ref[h] * e_end + lax.dot_general(
                        k_dec, v_new[h], tn_dims, preferred_element_type=F32)
                yield
            for h0 in range(0, N_HEADS, piece):
                for h in range(h0, h0 + piece):
                    o = _dot(uw[ci, h][1], jnp.concatenate([s_bf[h], v_new[h]], axis=0))
                    if add_first:
                        o = o + ofwd_ref[rows[ci], cols[h]]
                    o_ref[rows[ci], cols[h]] = o
                yield

    order = list(range(nch - 1, -1, -1) if reverse else range(nch))
    groups = [order[n:n + DELTA_GROUP_CHUNKS] for n in range(0, nch, DELTA_GROUP_CHUNKS)]
    stages = (prep_stage, inverse_stage, scan_stage)
    done = object()
    for tick in range(len(groups) + len(stages) - 1):
        running = [stage(groups[tick - d]) for d, stage in enumerate(stages)
                   if 0 <= tick - d < len(groups)]
        while running:
            running = [g for g in running if next(g, done) is not done]


def _delta(q, k, v, gates, reverse, ofwd=None):
    b, t, _ = q.shape
    c = DELTA_CHUNK * min(DELTA_STEP_CHUNKS, t // DELTA_CHUNK)
    nc = t // c
    add_first = ofwd is not None
    if reverse:
        rows = lambda bi, i: (bi, nc - 1 - i, 0)
    else:
        rows = lambda bi, i: (bi, i, 0)
    wide = pl.BlockSpec((None, c, QK_W), rows)
    in_specs = [wide, wide, wide, pl.BlockSpec((None, c, LANES), rows)]
    args = [q, k, v, gates]
    if add_first:
        in_specs += [wide]
        args += [ofwd]
    return pl.pallas_call(
        functools.partial(_delta_kernel, reverse=reverse, add_first=add_first),
        grid=(b, nc),
        in_specs=in_specs,
        out_specs=wide,
        out_shape=jax.ShapeDtypeStruct((b, t, QK_W), F32),
        scratch_shapes=[pltpu.VMEM((N_HEADS, HEAD_D, HEAD_D), F32)],
        compiler_params=pltpu.CompilerParams(dimension_semantics=("arbitrary", "arbitrary")),
        name="delta_bwd" if reverse else "delta_fwd",
    )(*args)


def _out_proj_kernel(o_ref_in, sz_ref, onw_ref, f_ref, szf_ref, sga_ref, sgb_ref, x_ref,
                     wpa_ref, wpb_ref, wo_ref, fnw_ref, o_ref, og_ref, *, final):
    ones = jnp.ones((HEAD_D, HEAD_D), BF16)
    onw = onw_ref[...]
    for h in range(N_HEADS):
        cols = slice(h * HEAD_D, (h + 1) * HEAD_D)
        oh = o_ref_in[:, cols]
        ms = _dot((oh * oh).astype(BF16), ones) * (1.0 / HEAD_D)
        og_ref[:, cols] = (oh * lax.rsqrt(ms + EPS) * onw * sz_ref[:, cols].astype(F32)).astype(BF16)
    ya = _dot(og_ref[...], wpa_ref[...])
    fg =(f_ref[...] * szf_ref[...].astype(F32)).astype(BF16)
    yb = _dot(fg, wpb_ref[...])
    merged = sga_ref[...].astype(F32) * ya + sgb_ref[...].astype(F32) * yb
    xo = x_ref[...] + _dot(merged.astype(BF16), wo_ref[...])
    if final:
        ms = jnp.mean(xo * xo, axis=-1, keepdims=True)
        xo = xo * lax.rsqrt(ms + EPS) * fnw_ref[...]
    o_ref[...] = xo


def _out_proj(o, sz, onw, f, szf, sga, sgb, x, wpa, wpb, wo, fnw, final):
    b, t, d = x.shape
    tb = min(ROW_BLOCK, t)
    rows = lambda bi, i: (bi, i, 0)
    const = lambda bi, i: (0, 0)
    row_spec = lambda w: pl.BlockSpec((None, tb, w), rows)
    full = lambda a: pl.BlockSpec(a.shape, const)
    return pl.pallas_call(
        functools.partial(_out_proj_kernel, final=final),
        grid=(b, t // tb),
        in_specs=[row_spec(QK_W), row_spec(QK_W), full(onw), row_spec(F_W), row_spec(F_W),
                  row_spec(d), row_spec(d), row_spec(d),
                  full(wpa), full(wpb), full(wo), full(fnw)],
        out_specs=row_spec(d),
        out_shape=jax.ShapeDtypeStruct((b, t, d), F32),
        scratch_shapes=[pltpu.VMEM((tb, QK_W), BF16)],
        compiler_params=pltpu.CompilerParams(
            dimension_semantics=("parallel", "parallel"), vmem_limit_bytes=VMEM_LIMIT),
        name="out_proj",
    )(o, sz, onw, f, szf, sga, sgb, x, wpa, wpb, wo, fnw)


def _pad_lanes(v):
    flat = v.reshape(1, -1).astype(F32)
    return jnp.pad(flat, ((0, 0), (0, LANES - flat.shape[1])))


def _layer_params(norm_w, w_in, conv_w, a_log, dt_bias, o_norm_w, w_pa, w_pb, w_o):
    o_z = 3 * QK_W
    o_ab = o_z + QK_W
    o_uf = o_ab + 2 * N_DIR * N_HEADS
    wqkv = w_in[:, :o_z].astype(BF16)
    wrest = jnp.concatenate([w_in[:, o_z:o_ab], w_in[:, o_uf:]], axis=1).astype(BF16)
    wab = jnp.pad(w_in[:, o_ab:o_uf], ((0, 0), (0, LANES - 2 * N_DIR * N_HEADS)))
    wabh = wab.astype(BF16)
    wab = jnp.concatenate([wabh, (wab - wabh.astype(F32)).astype(BF16)], axis=1)
    convw = jnp.pad(conv_w, ((0, SUBLANES - CONV_K), (0, 0)))
    return dict(norm_w=norm_w.reshape(1, -1), wqkv=wqkv, wrest=wrest, wab=wab,
                convw=convw, alog=_pad_lanes(a_log), dtb=_pad_lanes(dt_bias),
                onw=o_norm_w.reshape(1, -1), wpa=w_pa.astype(BF16), wpb=w_pb.astype(BF16),
                wo=w_o.astype(BF16))


def _layer(x, p, fnw, final):
    q, k, v, sz, gates, uf, szf, sga, sgb = _in_proj(
        x, p["norm_w"], p["wqkv"], p["wrest"], p["wab"], p["convw"], p["alog"], p["dtb"])
    f = _fft_real(uf)
    o_fwd = _delta(q, k, v, gates, reverse=False)
    o = _delta(q, k, v, gates, reverse=True, ofwd=o_fwd)
    return _out_proj(o, sz, p["onw"], f, szf, sga, sgb, x, p["wpa"], p["wpb"], p["wo"], fnw, final)


def _trunk(x, params, fnw):
    depth = len(params)
    for l, p in enumerate(params):
        x = _layer(x, p, fnw, final=(l == depth - 1))
    return x


def kernel(x_prompt, x_sample, norm_w, w_in, conv_w, a_log, dt_bias, o_norm_w, w_pa, w_pb, w_o,
           final_norm_w):
    depth = norm_w.shape[0]
    params = [_layer_params(norm_w[l], w_in[l], conv_w[l], a_log[l], dt_bias[l], o_norm_w[l],
                            w_pa[l], w_pb[l], w_o[l]) for l in range(depth)]
    fnw = final_norm_w.reshape(1, -1)
    return (_trunk(x_prompt, params, fnw), _trunk(x_sample, params, fnw))
```

```python
import functools
import math

import numpy as np
import jax
import jax.numpy as jnp
from jax import lax
from jax.experimental import pallas as pl
from jax.experimental.pallas import tpu as pltpu

D_MODEL = 1024
N_HEADS = 8
HEAD_D = 128
QK_W = N_HEADS * HEAD_D
CONV_K = 5
CONV_PAD = CONV_K // 2
N_DIR = 2
F_GROUPS = 4
F_GROUP_W = 128
F_W = F_GROUPS * F_GROUP_W
EPS = 1e-6

LANES = 128
SUBLANES = 8
MXU_N = 256
DELTA_CHUNK = 128
DELTA_LEVELS = 7
DELTA_STEP_CHUNKS = 8
DELTA_GROUP_CHUNKS = 2
BF16_TILE_ROWS = 16
ROW_BLOCK = 512
VMEM_LIMIT = 56 * 1024 * 1024

G_LANE = 0
B_LANE = N_DIR * N_HEADS

BF16 = jnp.bfloat16
F32 = jnp.float32


def _dot(a, b):
    return jnp.dot(a, b, preferred_element_type=F32)


def _sigmoid(x):
    return 0.5 + 0.5 * jnp.tanh(0.5 * x)


def _silu(x):
    h = 0.5 * x
    return h + h * jnp.tanh(h)


def _in_proj_kernel(x_ref, xp_ref, xn_ref, nw_ref, wqkv_ref, wrest_ref, wab_ref,
                    convw_ref, alog_ref, dtb_ref,
                    q_ref, k_ref, v_ref, z_ref, gate_ref, uf_ref, zf_ref, ga_ref, gb_ref,
                    pext_ref):
    i = pl.program_id(1)
    n = pl.num_programs(1)
    tb = x_ref.shape[0]
    nw = nw_ref[...]

    def norm(xb):
        ms = jnp.mean(xb * xb, axis=-1, keepdims=True)
        return xb * lax.rsqrt(ms + EPS) * nw

    xm = norm(x_ref[...])
    xp = norm(xp_ref[...]) * jnp.where(i > 0, 1.0, 0.0)
    xn = norm(xn_ref[...]) * jnp.where(i < n - 1, 1.0, 0.0)
    xm_bf = xm.astype(BF16)
    xext_bf = jnp.concatenate([xp, xm, xn], axis=0).astype(BF16)

    ones = jnp.ones((HEAD_D, HEAD_D), BF16)
    row0 = SUBLANES - CONV_PAD
    head_cols = [slice(h * HEAD_D, (h + 1) * HEAD_D) for h in range(N_HEADS)]

    def project(seg, hp):
        lo, hi = 2 * hp * HEAD_D, 2 * (hp + 1) * HEAD_D
        pext_ref[seg % 2, :, lo:hi] = _dot(xext_bf, wqkv_ref[:, seg * QK_W + lo:seg * QK_W + hi])

    def conv_head(seg, h, out_ref):
        wcols = slice(seg * QK_W + h * HEAD_D, seg * QK_W + (h + 1) * HEAD_D)
        p = pext_ref[seg % 2, :, head_cols[h]]
        acc = None
        for j in range(CONV_K):
            shift = (SUBLANES - row0 - j) % p.shape[0]
            tap = p if shift == 0 else pltpu.roll(p, shift, axis=0)
            term = convw_ref[j:j + 1, wcols] * tap[SUBLANES:SUBLANES + tb]
            acc = term if acc is None else acc + term
        s = _silu(acc)
        if seg < 2:
            ss = _dot((s * s).astype(BF16), ones)
            inv = lax.rsqrt(ss + EPS)
            s = s * (inv * (HEAD_D ** -0.5) if seg == 0 else inv)
        out_ref[:, head_cols[h]] = s.astype(out_ref.dtype)

    rest = []
    for out_ref in (z_ref, uf_ref, zf_ref, ga_ref, gb_ref):
        rest += [(out_ref, n) for n in range(out_ref.shape[1] // MXU_N)]

    def rest_piece(n):
        out_ref, m = rest[n]
        y = _dot(xm_bf, wrest_ref[:, n * MXU_N:(n + 1) * MXU_N])
        out_ref[:, m * MXU_N:(m + 1) * MXU_N] = y.astype(out_ref.dtype)

    pairs = N_HEADS // 2
    for hp in range(pairs):
        project(0, hp)
    rest_pieces = iter(range(len(rest)))
    for seg, out_ref in enumerate((q_ref, k_ref, v_ref)):
        for h in range(N_HEADS):
            conv_head(seg, h, out_ref)
            if seg < 2 and h < pairs:
                project(seg + 1, h)
            else:
                rest_piece(next(rest_pieces))
    for n in rest_pieces:
        rest_piece(n)

    xm_lo = (xm - xm_bf.astype(F32)).astype(BF16)
    hi = _dot(xm_bf, wab_ref[...])
    ab = hi[:, :LANES] + hi[:, LANES:] + _dot(xm_lo, wab_ref[:, :LANES])
    lane = lax.broadcasted_iota(jnp.int32, ab.shape, 1)
    xs = ab + dtb_ref[...]
    softplus = jnp.maximum(xs, 0.0) + jnp.log(1.0 + jnp.exp(-jnp.abs(xs)))
    g = jnp.where(lane < B_LANE, -jnp.exp(alog_ref[...]) * softplus, 0.0)
    beta = _sigmoid(ab)

    c = DELTA_CHUNK
    r = lax.broadcasted_iota(jnp.int32, (c, c), 0)
    cidx = lax.broadcasted_iota(jnp.int32, (c, c), 1)
    tril = jnp.where(r >= cidx, 1.0, 0.0).astype(F32)
    ln = lax.broadcasted_iota(jnp.int32, (c, LANES), 1)
    for ci in range(tb // c):
        rows = slice(ci * c, (ci + 1) * c)
        gch = g[rows]
        cf = jnp.dot(tril, gch, precision=lax.Precision.HIGHEST, preferred_element_type=F32)
        cb = cf[c - 1:c, :] - cf + gch
        out = jnp.where(ln < N_HEADS, cf,
                        jnp.where(ln < B_LANE, cb,
                                  jnp.where(ln < 2 * B_LANE, beta[rows], 0.0)))
        gate_ref[rows, :] = out


def _in_proj(x, norm_w, wqkv, wrest, wab, convw, alog, dtb):
    b, t, d = x.shape
    tb = min(ROW_BLOCK, t)
    nb = t // tb
    r8 = tb // SUBLANES
    last8 = t // SUBLANES - 1

    def rows(bi, i):
        return (bi, i, 0)

    def const(bi, i):
        return (0, 0)

    row_spec = lambda w: pl.BlockSpec((None, tb, w), rows)
    full = lambda a: pl.BlockSpec(a.shape, const)
    out_shapes = (
        jax.ShapeDtypeStruct((b, t, QK_W), BF16),
        jax.ShapeDtypeStruct((b, t, QK_W), BF16),
        jax.ShapeDtypeStruct((b, t, QK_W), BF16),
        jax.ShapeDtypeStruct((b, t, QK_W), BF16),
        jax.ShapeDtypeStruct((b, t, LANES), F32),
        jax.ShapeDtypeStruct((b, t, F_W), BF16),
        jax.ShapeDtypeStruct((b, t, F_W), BF16),
        jax.ShapeDtypeStruct((b, t, D_MODEL), BF16),
        jax.ShapeDtypeStruct((b, t, D_MODEL), BF16),
    )
    return pl.pallas_call(
        _in_proj_kernel,
        grid=(b, nb),
        in_specs=[
            row_spec(d),
            pl.BlockSpec((None, SUBLANES, d), lambda bi, i: (bi, jnp.maximum(i * r8 - 1, 0), 0)),
            pl.BlockSpec((None, SUBLANES, d), lambda bi, i: (bi, jnp.minimum((i + 1) * r8, last8), 0)),
            full(norm_w), full(wqkv), full(wrest), full(wab), full(convw),
            full(alog), full(dtb),
        ],
        out_specs=[row_spec(s.shape[-1]) for s in out_shapes],
        out_shape=out_shapes,
        scratch_shapes=[pltpu.VMEM((2, tb + 2 * SUBLANES, QK_W), F32)],
        compiler_params=pltpu.CompilerParams(
            dimension_semantics=("parallel", "parallel"), vmem_limit_bytes=VMEM_LIMIT),
        name="in_proj",
    )(x, x, x, norm_w, wqkv, wrest, wab, convw, alog, dtb)


@functools.lru_cache(maxsize=None)
def _fft_tables(n1, n2):
    t = n1 * n2
    c = np.arange(F_GROUP_W)
    ang = 2.0 * np.pi * ((c[:, None] * c[None, :]) % F_GROUP_W) / F_GROUP_W
    sc = 1.0 / math.sqrt(F_GROUP_W)
    wc = np.concatenate([np.cos(ang), -np.sin(ang)], axis=1) * sc
    a = np.arange(n1)
    ang1 = 2.0 * np.pi * ((a[:, None] * a[None, :]) % n1) / n1
    fr, fi = np.cos(ang1) / math.sqrt(n1), -np.sin(ang1) / math.sqrt(n1)
    m1 = np.block([[fr, -fi], [fi, fr]])
    t1p = np.arange(n1)[:, None, None]
    t2p = np.arange(n2)[None, :, None]
    t2 = np.arange(n2)[None, None, :]
    ang2 = 2.0 * np.pi * ((t2 * (t1p + n1 * t2p)) % t) / t
    g = np.concatenate([np.cos(ang2), np.sin(ang2)], axis=2) / math.sqrt(n2)
    as_bf = lambda m: m.astype(np.float32).astype(BF16)
    return as_bf(wc), as_bf(m1), as_bf(g)


def _fft_a_kernel(u_ref, wc_ref, m1_ref, yr_ref, yi_ref):
    n1 = u_ref.shape[0]
    wc = wc_ref[...]
    m1 = m1_ref[...]
    for tt in range(yr_ref.shape[1]):
        ub = u_ref[:, tt * F_W:(tt + 1) * F_W]
        zr, zi = [], []
        for gi in range(F_GROUPS):
            z = _dot(ub[:, gi * F_GROUP_W:(gi + 1) * F_GROUP_W], wc)
            zr.append(z[:, :F_GROUP_W])
            zi.append(z[:, F_GROUP_W:])
        zst = jnp.concatenate([jnp.concatenate(zr, axis=1), jnp.concatenate(zi, axis=1)], axis=0)
        y = _dot(m1, zst.astype(BF16))
        yr_ref[:, tt, :] = y[:n1]
        yi_ref[:, tt, :] = y[n1:]


def _fft_b_kernel(yr_ref, yi_ref, g_ref, f_ref):
    n2 = g_ref.shape[1]
    for j in range(g_ref.shape[0]):
        rows = slice(j * n2, (j + 1) * n2)
        yst = jnp.concatenate([yr_ref[rows, :], yi_ref[rows, :]], axis=0).astype(BF16)
        f_ref[:, j, :] = _dot(g_ref[j], yst)


def _fft_real(uf):
    b, t, _ = uf.shape
    n1 = 1 << ((t.bit_length() - 1) // 2)
    n2 = t // n1
    wc, m1, g = _fft_tables(n1, n2)
    u2 = uf.reshape(b, n1, n2 * F_W)
    blk_a = pl.BlockSpec((None, n1, SUBLANES, F_W), lambda bi, i: (bi, 0, i, 0))
    yr, yi = pl.pallas_call(
        _fft_a_kernel,
        grid=(b, n2 // SUBLANES),
        in_specs=[pl.BlockSpec((None, n1, SUBLANES * F_W), lambda bi, i: (bi, 0, i)),
                  pl.BlockSpec(wc.shape, lambda bi, i: (0, 0)),
                  pl.BlockSpec(m1.shape, lambda bi, i: (0, 0))],
        out_specs=[blk_a, blk_a],
        out_shape=(jax.ShapeDtypeStruct((b, n1, n2, F_W), F32),) * 2,
        compiler_params=pltpu.CompilerParams(dimension_semantics=("parallel", "parallel")),
        name="fft_a",
    )(u2, wc, m1)
    yr = yr.reshape(b, t, F_W)
    yi = yi.reshape(b, t, F_W)
    blk_y = pl.BlockSpec((None, SUBLANES * n2, F_W), lambda bi, i: (bi, i, 0))
    f = pl.pallas_call(
        _fft_b_kernel,
        grid=(b, n1 // SUBLANES),
        in_specs=[blk_y, blk_y,
                  pl.BlockSpec((SUBLANES, n2, 2 * n2), lambda bi, i: (i, 0, 0))],
        out_specs=pl.BlockSpec((None, n2, SUBLANES, F_W), lambda bi, i: (bi, 0, i, 0)),
        out_shape=jax.ShapeDtypeStruct((b, n2, n1, F_W), F32),
        compiler_params=pltpu.CompilerParams(dimension_semantics=("parallel", "parallel")),
        name="fft_b",
    )(yr, yi, g)
    return f.reshape(b, t, F_W)


def _delta_kernel(*refs, reverse, add_first):
    if add_first:
        q_ref, k_ref, v_ref, gate_ref, ofwd_ref, o_ref, s_ref = refs
    else:
        q_ref, k_ref, v_ref, gate_ref, o_ref, s_ref = refs
    c = DELTA_CHUNK

    @pl.when(pl.program_id(1) == 0)
    def _():
        s_ref[...] = jnp.zeros_like(s_ref)

    nch = q_ref.shape[0] // c
    row = lax.broadcasted_iota(jnp.int32, (c, c), 0)
    col = lax.broadcasted_iota(jnp.int32, (c, c), 1)
    if reverse:
        valid, strict, last = row <= col, row < col, 0
    else:
        valid, strict, last = row >= col, row > col, c - 1
    eye =jnp.where(row == col, 1.0, 0.0).astype(F32)
    level_masks = [((row >> l) ^ (col >> l)) == 1 for l in range(DELTA_LEVELS)]
    nt_dims = (((1,), (1,)), ((), ()))
    tn_dims = (((0,), (0,)), ((), ()))

    heads = range(N_HEADS)
    rows = [slice(ci * c, (ci + 1) * c) for ci in range(nch)]
    cols = [slice(h * HEAD_D, (h + 1) * HEAD_D) for h in heads]
    dir_off = N_HEADS if reverse else 0
    g_lane = [G_LANE + dir_off + h for h in heads]
    b_lane = [B_LANE + dir_off + h for h in heads]
    gates, lmat, attn, x, uw = {}, {}, {}, {}, {}

    def lane_bcast(m, lane):
        return jnp.broadcast_to(m[:, lane:lane + 1], (c, c))

    def prep_stage(group):
        for ci in group:
            gcol = gate_ref[rows[ci], :]
            g_end = gcol[last:last + 1, :]
            gt = dict(col=gcol, row=gcol.T, e=jnp.exp(gcol), e_end=jnp.exp(g_end),
                      e_rest=jnp.exp(jnp.minimum(g_end - gcol, 0.0)))
            gates[ci] = gt
            yield
            for h in heads:
                g_i = lane_bcast(gt["col"], g_lane[h])
                g_j = jnp.broadcast_to(gt["row"][g_lane[h]:g_lane[h] + 1, :], (c, c))
                decay = jnp.exp(jnp.where(valid, g_i - g_j, -1e30))
                k = k_ref[rows[ci], cols[h]]
                qk = jnp.concatenate([q_ref[rows[ci], cols[h]], k], axis=0)
                gram = lax.dot_general(qk, k, nt_dims, preferred_element_type=F32)
                attn[ci, h] = (gram[:c] * decay).astype(BF16)
                lmat[ci, h] = jnp.where(
                    strict, lane_bcast(gt["col"], b_lane[h]) * gram[c:] * decay, 0.0)
                yield

    def take_rows(m, blocks, active):
        return jnp.concatenate([m[blk] for blk, a in zip(blocks, active) if a], axis=0)

    def put_rows(part, rest, blocks, active):
        out, n = [], 0
        for blk, a in zip(blocks, active):
            size = blk.stop - blk.start
            if a:
                piece = part[n * size:(n + 1) * size]
                out.append(piece if rest is None else rest[blk] - piece)
                n += 1
            else:
                out.append(jnp.zeros((size, c), part.dtype) if rest is None else rest[blk])
        return jnp.concatenate(out, axis=0)

    def inverse_stage(group):
        pairs = [(ci, h) for ci in group for h in heads]
        for p in pairs:
            x[p] = eye - jnp.where(level_masks[0], lmat[p], 0.0)
        yield
        for l in range(1, DELTA_LEVELS):
            size = 1 << l
            full = size < BF16_TILE_ROWS
            blocks = [slice(bi * size, (bi + 1) * size) for bi in range(c // size)]
            active = [full or (bi % 2 == (0 if reverse else 1)) for bi in range(c // size)]
            cx = {}
            for p in pairs:
                cl = take_rows(jnp.where(level_masks[l], lmat[p], 0.0), blocks, active)
                cx[p] = put_rows(_dot(cl.astype(BF16), x[p].astype(BF16)).astype(BF16), None,
                                 blocks, active)
            yield
            for p in pairs:
                upd = _dot(take_rows(x[p], blocks, active).astype(BF16), cx[p])
                x[p] = put_rows(upd, x[p], blocks, active)
            yield
        for ci, h in pairs:
            gt = gates[ci]
            b_j = jnp.broadcast_to(gt["row"][b_lane[h]:b_lane[h] + 1, :], (c, c))
            e_i = lane_bcast(gt["e"], g_lane[h])
            kg = (k_ref[rows[ci], cols[h]].astype(F32) * e_i).astype(BF16)
            tb_ = (x[ci, h] * b_j).astype(BF16)
            neg_w = (-_dot(tb_, kg)).astype(BF16)
            q_dec = (q_ref[rows[ci], cols[h]].astype(F32) * e_i).astype(BF16)
            uw[ci, h] = (jnp.concatenate([tb_, neg_w], axis=1),
                         jnp.concatenate([q_dec, attn[ci, h]], axis=1))
        yield

    def scan_stage(group):
        piece = 2
        for ci in group:
            gt = gates[ci]
            s_bf, v_new = {}, {}
            for h0 in range(0, N_HEADS, piece):
                for h in range(h0, h0 + piece):
                    s_bf[h] = s_ref[h].astype(BF16)
                    vs = jnp.concatenate([v_ref[rows[ci], cols[h]], s_bf[h]], axis=0)
                    v_new[h] = _dot(uw[ci, h][0], vs).astype(BF16)
                yield
            for h0 in range(0, N_HEADS, piece):
                for h in range(h0, h0 + piece):
                    k_dec = (k_ref[rows[ci], cols[h]].astype(F32)
                             * lane_bcast(gt["e_rest"], g_lane[h])).astype(BF16)
                    e_end = jnp.broadcast_to(gt["e_end"][:, g_lane[h]:g_lane[h] + 1], (c, c))
                    s_ref[h] = s_ref[h] * e_end + lax.dot_general(
                        k_dec, v_new[h], tn_dims, preferred_element_type=F32)
                yield
            for h0 in range(0, N_HEADS, piece):
                for h in range(h0, h0 + piece):
                    o = _dot(uw[ci, h][1], jnp.concatenate([s_bf[h], v_new[h]], axis=0))
                    if add_first:
                        o = o + ofwd_ref[rows[ci], cols[h]]
                    o_ref[rows[ci], cols[h]] = o
                yield

    order = list(range(nch - 1, -1, -1) if reverse else range(nch))
    groups = [order[n:n + DELTA_GROUP_CHUNKS] for n in range(0, nch, DELTA_GROUP_CHUNKS)]
    stages = (prep_stage, inverse_stage, scan_stage)
    done = object()
    for tick in range(len(groups) + len(stages) - 1):
        running = [stage(groups[tick - d]) for d, stage in enumerate(stages)
                   if 0 <= tick - d < len(groups)]
        while running:
            running = [g for g in running if next(g, done) is not done]


def _delta(q, k, v, gates, reverse, ofwd=None):
    b, t, _ = q.shape
    c = DELTA_CHUNK * min(DELTA_STEP_CHUNKS, t // DELTA_CHUNK)
    nc = t // c
    add_first = ofwd is not None
    if reverse:
        rows = lambda bi, i: (bi, nc - 1 - i, 0)
    else:
        rows = lambda bi, i: (bi, i, 0)
    wide = pl.BlockSpec((None, c, QK_W), rows)
    in_specs = [wide, wide, wide, pl.BlockSpec((None, c, LANES), rows)]
    args = [q, k, v, gates]
    if add_first:
        in_specs += [wide]
        args += [ofwd]
    return pl.pallas_call(
        functools.partial(_delta_kernel, reverse=reverse, add_first=add_first),
        grid=(b, nc),
        in_specs=in_specs,
        out_specs=wide,
        out_shape=jax.ShapeDtypeStruct((b, t, QK_W), F32),
        scratch_shapes=[pltpu.VMEM((N_HEADS, HEAD_D, HEAD_D), F32)],
        compiler_params=pltpu.CompilerParams(dimension_semantics=("arbitrary", "arbitrary")),
        name="delta_bwd" if reverse else "delta_fwd",
    )(*args)


def _out_proj_kernel(o_ref_in, z_ref, onw_ref, f_ref, zf_ref, ga_ref, gb_ref, x_ref,
                     wpa_ref, wpb_ref, wo_ref, fnw_ref, o_ref, og_ref, *, final):
    ones = jnp.ones((HEAD_D, HEAD_D), BF16)
    onw = onw_ref[...]
    for h in range(N_HEADS):
        cols = slice(h * HEAD_D, (h + 1) * HEAD_D)
        oh = o_ref_in[:, cols]
        ms = _dot((oh * oh).astype(BF16), ones) * (1.0 / HEAD_D)
        gate = _silu(z_ref[:, cols].astype(F32))
        og_ref[:, cols] = (oh * lax.rsqrt(ms + EPS) * onw * gate).astype(BF16)
    ya = _dot(og_ref[...], wpa_ref[...])
    fg = (f_ref[...] * _silu(zf_ref[...].astype(F32))).astype(BF16)
    yb = _dot(fg, wpb_ref[...])
    merged = (_sigmoid(ga_ref[...].astype(F32)) * ya + _sigmoid(gb_ref[...].astype(F32)) * yb)
    xo = x_ref[...] + _dot(merged.astype(BF16), wo_ref[...])
    if final:
        ms = jnp.mean(xo * xo, axis=-1, keepdims=True)
        xo = xo * lax.rsqrt(ms + EPS) * fnw_ref[...]
    o_ref[...] = xo


def _out_proj(o, z, onw, f, zf, ga, gb, x, wpa, wpb, wo, fnw, final):
    b, t, d = x.shape
    tb = min(ROW_BLOCK, t)
    rows = lambda bi, i: (bi, i, 0)
    const = lambda bi, i: (0, 0)
    row_spec = lambda w: pl.BlockSpec((None, tb, w), rows)
    full = lambda a: pl.BlockSpec(a.shape, const)
    return pl.pallas_call(
        functools.partial(_out_proj_kernel, final=final),
        grid=(b, t // tb),
        in_specs=[row_spec(QK_W), row_spec(QK_W), full(onw), row_spec(F_W), row_spec(F_W),
                  row_spec(d), row_spec(d), row_spec(d),
                  full(wpa), full(wpb), full(wo), full(fnw)],
        out_specs=row_spec(d),
        out_shape=jax.ShapeDtypeStruct((b, t, d), F32),
        scratch_shapes=[pltpu.VMEM((tb, QK_W), BF16)],
        compiler_params=pltpu.CompilerParams(
            dimension_semantics=("parallel", "parallel"), vmem_limit_bytes=VMEM_LIMIT),
        name="out_proj",
    )(o, z, onw, f, zf, ga, gb, x, wpa, wpb, wo, fnw)


def _pad_lanes(v):
    flat = v.reshape(1, -1).astype(F32)
    return jnp.pad(flat, ((0, 0), (0, LANES - flat.shape[1])))


def _layer_params(norm_w, w_in, conv_w, a_log, dt_bias, o_norm_w, w_pa, w_pb, w_o):
    o_z = 3 * QK_W
    o_ab = o_z + QK_W
    o_uf = o_ab + 2 * N_DIR * N_HEADS
    wqkv = w_in[:, :o_z].astype(BF16)
    wrest = jnp.concatenate([w_in[:, o_z:o_ab], w_in[:, o_uf:]], axis=1).astype(BF16)
    wab = jnp.pad(w_in[:, o_ab:o_uf], ((0, 0), (0, LANES - 2 * N_DIR * N_HEADS)))
    wabh = wab.astype(BF16)
    wab = jnp.concatenate([wabh, (wab - wabh.astype(F32)).astype(BF16)], axis=1)
    convw = jnp.pad(conv_w, ((0, SUBLANES - CONV_K), (0, 0)))
    return dict(norm_w=norm_w.reshape(1, -1), wqkv=wqkv, wrest=wrest, wab=wab,
                convw=convw, alog=_pad_lanes(a_log), dtb=_pad_lanes(dt_bias),
                onw=o_norm_w.reshape(1, -1), wpa=w_pa.astype(BF16), wpb=w_pb.astype(BF16),
                wo=w_o.astype(BF16))


def _layer(x, p, fnw, final):
    q, k, v, z, gates, uf, zf, ga, gb = _in_proj(
        x, p["norm_w"], p["wqkv"], p["wrest"], p["wab"], p["convw"], p["alog"], p["dtb"])
    f = _fft_real(uf)
    o_fwd = _delta(q, k, v, gates, reverse=False)
    o = _delta(q, k, v, gates, reverse=True, ofwd=o_fwd)
    return _out_proj(o, z, p["onw"], f, zf, ga, gb, x, p["wpa"], p["wpb"], p["wo"], fnw, final)


def _trunk(x, params, fnw):
    depth = len(params)
    for l, p in enumerate(params):
        x = _layer(x, p, fnw, final=(l == depth - 1))
    return x


def kernel(x_prompt, x_sample, norm_w, w_in, conv_w, a_log, dt_bias, o_norm_w, w_pa, w_pb, w_o,
           final_norm_w):
    depth = norm_w.shape[0]
    params = [_layer_params(norm_w[l], w_in[l], conv_w[l], a_log[l], dt_bias[l], o_norm_w[l],
                            w_pa[l], w_pb[l], w_o[l]) for l in range(depth)]
    fnw = final_norm_w.reshape(1, -1)
    return (_trunk(x_prompt, params, fnw), _trunk(x_sample, params, fnw))
```

```python
import functools
import math

import numpy as np
import jax
import jax.numpy as jnp
from jax import lax
from jax.experimental import pallas as pl
from jax.experimental.pallas import tpu as pltpu

D_MODEL = 1024
N_HEADS = 8
HEAD_D = 128
QK_W = N_HEADS * HEAD_D
CONV_K = 5
CONV_PAD = CONV_K // 2
N_DIR = 2
F_GROUPS = 4
F_GROUP_W = 128
F_W = F_GROUPS * F_GROUP_W
EPS = 1e-6

LANES = 128
SUBLANES = 8
MXU_N = 256
DELTA_CHUNK = 128
DELTA_LEVELS = 7
DELTA_STEP_CHUNKS = 8
DELTA_GROUP_CHUNKS = 2
BF16_TILE_ROWS = 16
ROW_BLOCK = 512
VMEM_LIMIT = 56 * 1024 * 1024

G_LANE = 0
B_LANE = N_DIR * N_HEADS

BF16 = jnp.bfloat16
F32 = jnp.float32


def _dot(a, b):
    return jnp.dot(a, b, preferred_element_type=F32)


def _sigmoid(x):
    return 0.5 + 0.5 * jnp.tanh(0.5 * x)


def _silu(x):
    h = 0.5 * x
    return h + h * jnp.tanh(h)


def _in_proj_kernel(x_ref, xp_ref, xn_ref, nw_ref, wqkv_ref, wrest_ref, wab_ref,
                    convw_ref, alog_ref, dtb_ref,
                    q_ref, k_ref, v_ref, z_ref, gate_ref, uf_ref, zf_ref, ga_ref, gb_ref,
                    pext_ref):
    i = pl.program_id(1)
    n = pl.num_programs(1)
    tb = x_ref.shape[0]
    nw = nw_ref[...]

    def norm(xb):
        ms = jnp.mean(xb * xb, axis=-1, keepdims=True)
        return xb * lax.rsqrt(ms + EPS) * nw

    xm = norm(x_ref[...])
    xp = norm(xp_ref[...]) * jnp.where(i > 0, 1.0, 0.0)
    xn = norm(xn_ref[...]) * jnp.where(i < n - 1, 1.0, 0.0)
    xm_bf = xm.astype(BF16)
    xext_bf = jnp.concatenate([xp, xm, xn], axis=0).astype(BF16)

    ones = jnp.ones((HEAD_D, HEAD_D), BF16)
    row0 = SUBLANES - CONV_PAD
    head_cols = [slice(h * HEAD_D, (h + 1) * HEAD_D) for h in range(N_HEADS)]

    def project(seg, hp):
        lo, hi = 2 * hp * HEAD_D, 2 * (hp + 1) * HEAD_D
        pext_ref[seg % 2, :, lo:hi] = _dot(xext_bf, wqkv_ref[:, seg * QK_W + lo:seg * QK_W + hi])

    def conv_head(seg, h, out_ref):
        wcols = slice(seg * QK_W + h * HEAD_D, seg * QK_W + (h + 1) * HEAD_D)
        p = pext_ref[seg % 2, :, head_cols[h]]
        acc = None
        for j in range(CONV_K):
            shift = (SUBLANES - row0 - j) % p.shape[0]
            tap = p if shift == 0 else pltpu.roll(p, shift, axis=0)
            term = convw_ref[j:j + 1, wcols] * tap[SUBLANES:SUBLANES + tb]
            acc = term if acc is None else acc + term
        s = _silu(acc)
        if seg < 2:
            ss = _dot((s * s).astype(BF16), ones)
            inv = lax.rsqrt(ss + EPS)
            s = s * (inv * (HEAD_D ** -0.5) if seg == 0 else inv)
        out_ref[:, head_cols[h]] = s.astype(out_ref.dtype)

    rest = []
    for out_ref in (z_ref, uf_ref, zf_ref, ga_ref, gb_ref):
        rest += [(out_ref, n) for n in range(out_ref.shape[1] // MXU_N)]

    def rest_piece(n):
        out_ref, m = rest[n]
        y = _dot(xm_bf, wrest_ref[:, n * MXU_N:(n + 1) * MXU_N])
        out_ref[:, m * MXU_N:(m + 1) * MXU_N] = y.astype(out_ref.dtype)

    pairs = N_HEADS // 2
    for hp in range(pairs):
        project(0, hp)
    rest_pieces = iter(range(len(rest)))
    for seg, out_ref in enumerate((q_ref, k_ref, v_ref)):
        for h in range(N_HEADS):
            conv_head(seg, h, out_ref)
            if seg < 2 and h < pairs:
                project(seg + 1, h)
            else:
                rest_piece(next(rest_pieces))
    for n in rest_pieces:
        rest_piece(n)

    xm_lo = (xm - xm_bf.astype(F32)).astype(BF16)
    hi = _dot(xm_bf, wab_ref[...])
    ab = hi[:, :LANES] + hi[:, LANES:] + _dot(xm_lo, wab_ref[:, :LANES])
    lane = lax.broadcasted_iota(jnp.int32, ab.shape, 1)
    xs = ab + dtb_ref[...]
    softplus = jnp.maximum(xs, 0.0) + jnp.log(1.0 + jnp.exp(-jnp.abs(xs)))
    g = jnp.where(lane < B_LANE, -jnp.exp(alog_ref[...]) * softplus, 0.0)
    beta = _sigmoid(ab)

    c = DELTA_CHUNK
    r = lax.broadcasted_iota(jnp.int32, (c, c), 0)
    cidx = lax.broadcasted_iota(jnp.int32, (c, c), 1)
    tril = jnp.where(r >= cidx, 1.0, 0.0).astype(F32)
    ln = lax.broadcasted_iota(jnp.int32, (c, LANES), 1)
    for ci in range(tb // c):
        rows = slice(ci * c, (ci + 1) * c)
        gch = g[rows]
        cf = jnp.dot(tril, gch, precision=lax.Precision.HIGHEST, preferred_element_type=F32)
        cb = cf[c - 1:c, :] - cf + gch
        out = jnp.where(ln < N_HEADS, cf,
                        jnp.where(ln < B_LANE, cb,
                                  jnp.where(ln < 2 * B_LANE, beta[rows], 0.0)))
        gate_ref[rows, :] = out


def _in_proj(x, norm_w, wqkv, wrest, wab, convw, alog, dtb):
    b, t, d = x.shape
    tb = min(ROW_BLOCK, t)
    nb = t // tb
    r8 = tb // SUBLANES
    last8 = t // SUBLANES - 1

    def rows(bi, i):
        return (bi, i, 0)

    def const(bi, i):
        return (0, 0)

    row_spec = lambda w: pl.BlockSpec((None, tb, w), rows)
    full = lambda a: pl.BlockSpec(a.shape, const)
    out_shapes = (
        jax.ShapeDtypeStruct((b, t, QK_W), BF16),
        jax.ShapeDtypeStruct((b, t, QK_W), BF16),
        jax.ShapeDtypeStruct((b, t, QK_W), BF16),
        jax.ShapeDtypeStruct((b, t, QK_W), BF16),
        jax.ShapeDtypeStruct((b, t, LANES), F32),
        jax.ShapeDtypeStruct((b, t, F_W), BF16),
        jax.ShapeDtypeStruct((b, t, F_W), BF16),
        jax.ShapeDtypeStruct((b, t, D_MODEL), BF16),
        jax.ShapeDtypeStruct((b, t, D_MODEL), BF16),
    )
    return pl.pallas_call(
        _in_proj_kernel,
        grid=(b, nb),
        in_specs=[
            row_spec(d),
            pl.BlockSpec((None, SUBLANES, d), lambda bi, i: (bi, jnp.maximum(i * r8 - 1, 0), 0)),
            pl.BlockSpec((None, SUBLANES, d), lambda bi, i: (bi, jnp.minimum((i + 1) * r8, last8), 0)),
            full(norm_w), full(wqkv), full(wrest), full(wab), full(convw),
            full(alog), full(dtb),
        ],
        out_specs=[row_spec(s.shape[-1]) for s in out_shapes],
        out_shape=out_shapes,
        scratch_shapes=[pltpu.VMEM((2, tb + 2 * SUBLANES, QK_W), F32)],
        compiler_params=pltpu.CompilerParams(
            dimension_semantics=("parallel", "parallel"), vmem_limit_bytes=VMEM_LIMIT),
        name="in_proj",
    )(x, x, x, norm_w, wqkv, wrest, wab, convw, alog, dtb)


@functools.lru_cache(maxsize=None)
def _fft_tables(n1, n2):
    t = n1 * n2
    c = np.arange(F_GROUP_W)
    ang = 2.0 * np.pi * ((c[:, None] * c[None, :]) % F_GROUP_W) / F_GROUP_W
    sc = 1.0 / math.sqrt(F_GROUP_W)
    wc = np.concatenate([np.cos(ang), -np.sin(ang)], axis=1) * sc
    a = np.arange(n1)
    ang1 = 2.0 * np.pi * ((a[:, None] * a[None, :]) % n1) / n1
    fr, fi = np.cos(ang1) / math.sqrt(n1), -np.sin(ang1) / math.sqrt(n1)
    m1 = np.block([[fr, -fi], [fi, fr]])
    t1p = np.arange(n1)[:, None, None]
    t2p = np.arange(n2)[None, :, None]
    t2 = np.arange(n2)[None, None, :]
    ang2 = 2.0 * np.pi * ((t2 * (t1p + n1 * t2p)) % t) / t
    g = np.concatenate([np.cos(ang2), np.sin(ang2)], axis=2) / math.sqrt(n2)
    as_bf = lambda m: m.astype(np.float32).astype(BF16)
    return as_bf(wc), as_bf(m1), as_bf(g)


def _fft_a_kernel(u_ref, wc_ref, m1_ref, yr_ref, yi_ref):
    n1 = u_ref.shape[0]
    wc = wc_ref[...]
    m1 = m1_ref[...]
    for tt in range(yr_ref.shape[1]):
        ub = u_ref[:, tt * F_W:(tt + 1) * F_W]
        zr, zi = [], []
        for gi in range(F_GROUPS):
            z = _dot(ub[:, gi * F_GROUP_W:(gi + 1) * F_GROUP_W], wc)
            zr.append(z[:, :F_GROUP_W])
            zi.append(z[:, F_GROUP_W:])
        zst = jnp.concatenate([jnp.concatenate(zr, axis=1), jnp.concatenate(zi, axis=1)], axis=0)
        y = _dot(m1, zst.astype(BF16))
        yr_ref[:, tt, :] = y[:n1]
        yi_ref[:, tt, :] = y[n1:]


def _fft_b_kernel(yr_ref, yi_ref, g_ref, f_ref):
    n2 = g_ref.shape[1]
    for j in range(g_ref.shape[0]):
        rows = slice(j * n2, (j + 1) * n2)
        yst = jnp.concatenate([yr_ref[rows, :], yi_ref[rows, :]], axis=0).astype(BF16)
        f_ref[:, j, :] = _dot(g_ref[j], yst)


def _fft_real(uf):
    b, t, _ = uf.shape
    n1 = 1 << ((t.bit_length() - 1) // 2)
    n2 = t // n1
    wc, m1, g = _fft_tables(n1, n2)
    u2 = uf.reshape(b, n1, n2 * F_W)
    blk_a = pl.BlockSpec((None, n1, SUBLANES, F_W), lambda bi, i: (bi, 0, i, 0))
    yr, yi = pl.pallas_call(
        _fft_a_kernel,
        grid=(b, n2 // SUBLANES),
        in_specs=[pl.BlockSpec((None, n1, SUBLANES * F_W), lambda bi, i: (bi, 0, i)),
                  pl.BlockSpec(wc.shape, lambda bi, i: (0, 0)),
                  pl.BlockSpec(m1.shape, lambda bi, i: (0, 0))],
        out_specs=[blk_a, blk_a],
        out_shape=(jax.ShapeDtypeStruct((b, n1, n2, F_W), F32),) * 2,
        compiler_params=pltpu.CompilerParams(dimension_semantics=("parallel", "parallel")),
        name="fft_a",
    )(u2, wc, m1)
    yr = yr.reshape(b, t, F_W)
    yi = yi.reshape(b, t, F_W)
    blk_y = pl.BlockSpec((None, SUBLANES * n2, F_W), lambda bi, i: (bi, i, 0))
    f = pl.pallas_call(
        _fft_b_kernel,
        grid=(b, n1 // SUBLANES),
        in_specs=[blk_y, blk_y,
                  pl.BlockSpec((SUBLANES, n2, 2 * n2), lambda bi, i: (i, 0, 0))],
        out_specs=pl.BlockSpec((None, n2, SUBLANES, F_W), lambda bi, i: (bi, 0, i, 0)),
        out_shape=jax.ShapeDtypeStruct((b, n2, n1, F_W), F32),
        compiler_params=pltpu.CompilerParams(dimension_semantics=("parallel", "parallel")),
        name="fft_b",
    )(yr, yi, g)
    return f.reshape(b, t, F_W)


def _delta_kernel(*refs, reverse, add_first):
    if add_first:
        q_ref, k_ref, v_ref, gate_ref, ofwd_ref, o_ref, s_ref = refs
    else:
        q_ref, k_ref, v_ref, gate_ref, o_ref, s_ref = refs
    c = DELTA_CHUNK

    @pl.when(pl.program_id(1) == 0)
    def _():
        s_ref[...] = jnp.zeros_like(s_ref)

    nch = q_ref.shape[0] // c
    row = lax.broadcasted_iota(jnp.int32, (c, c), 0)
    col = lax.broadcasted_iota(jnp.int32, (c, c), 1)
    if reverse:
        valid, strict, last = row <= col, row < col, 0
    else:
        valid, strict, last = row >= col, row > col, c - 1
    eye =jnp.where(row == col, 1.0, 0.0).astype(F32)
    level_masks = [((row >> l) ^ (col >> l)) == 1 for l in range(DELTA_LEVELS)]
    nt_dims = (((1,), (1,)), ((), ()))
    tn_dims = (((0,), (0,)), ((), ()))

    heads = range(N_HEADS)
    rows = [slice(ci * c, (ci + 1) * c) for ci in range(nch)]
    cols = [slice(h * HEAD_D, (h + 1) * HEAD_D) for h in heads]
    dir_off = N_HEADS if reverse else 0
    g_lane = [G_LANE + dir_off + h for h in heads]
    b_lane = [B_LANE + dir_off + h for h in heads]
    gates, lmat, attn, x, uw = {}, {}, {}, {}, {}

    def lane_bcast(m, lane):
        return jnp.broadcast_to(m[:, lane:lane + 1], (c, c))

    def prep_stage(group):
        for ci in group:
            gcol = gate_ref[rows[ci], :]
            g_end = gcol[last:last + 1, :]
            beta_at_g = pltpu.roll(gcol, LANES - B_LANE, axis=1)
            gt = dict(col=gcol, row=gcol.T, e=jnp.exp(gcol), e_end=jnp.exp(g_end),
                      e_rest=jnp.exp(jnp.minimum(g_end - gcol, 0.0)) * beta_at_g)
            gates[ci] = gt
            yield
            for h in heads:
                g_i = lane_bcast(gt["col"], g_lane[h])
                g_j = jnp.broadcast_to(gt["row"][g_lane[h]:g_lane[h] + 1, :], (c, c))
                decay = jnp.exp(jnp.where(valid, g_i - g_j, -1e30))
                k = k_ref[rows[ci], cols[h]]
                qk = jnp.concatenate([q_ref[rows[ci], cols[h]], k], axis=0)
                gram = lax.dot_general(qk, k, nt_dims, preferred_element_type=F32)
                decay_b = decay * jnp.broadcast_to(gt["row"][b_lane[h]:b_lane[h] + 1, :], (c, c))
                attn[ci, h] = (gram[:c] * decay_b).astype(BF16)
                lmat[ci, h] = jnp.where(strict, gram[c:] * decay_b, 0.0)
                yield

    def take_rows(m, blocks, active):
        return jnp.concatenate([m[blk] for blk, a in zip(blocks, active) if a], axis=0)

    def put_rows(part, rest, blocks, active):
        out, n = [], 0
        for blk, a in zip(blocks, active):
            size = blk.stop - blk.start
            if a:
                piece = part[n * size:(n + 1) * size]
                out.append(piece if rest is None else rest[blk] - piece)
                n += 1
            else:
                out.append(jnp.zeros((size, c), part.dtype) if rest is None else rest[blk])
        return jnp.concatenate(out, axis=0)

    def inverse_stage(group):
        pairs = [(ci, h) for ci in group for h in heads]
        for p in pairs:
            x[p] = eye - jnp.where(level_masks[0], lmat[p], 0.0)
        yield
        for l in range(1, DELTA_LEVELS):
            size = 1 << l
            full = size < BF16_TILE_ROWS
            blocks = [slice(bi * size, (bi + 1) * size) for bi in range(c // size)]
            active = [full or (bi % 2 == (0 if reverse else 1)) for bi in range(c // size)]
            cx = {}
            for p in pairs:
                cl = take_rows(jnp.where(level_masks[l], lmat[p], 0.0), blocks, active)
                cx[p] = put_rows(_dot(cl.astype(BF16), x[p].astype(BF16)).astype(BF16), None,
                                 blocks, active)
            yield
            for p in pairs:
                upd = _dot(take_rows(x[p], blocks, active).astype(BF16), cx[p])
                x[p] = put_rows(upd, x[p], blocks, active)
            yield
        for ci, h in pairs:
            gt = gates[ci]
            e_i = lane_bcast(gt["e"], g_lane[h])
            kg = (k_ref[rows[ci], cols[h]].astype(F32) * e_i).astype(BF16)
            tb_ = x[ci, h].astype(BF16)
            neg_w = (-_dot(tb_, kg)).astype(BF16)
            q_dec = (q_ref[rows[ci], cols[h]].astype(F32) * e_i).astype(BF16)
            uw[ci, h] = (jnp.concatenate([tb_, neg_w], axis=1),
                         jnp.concatenate([q_dec, attn[ci, h]], axis=1))
        yield

    def scan_stage(group):
        piece = 2
        for ci in group:
            gt = gates[ci]
            s_bf, v_new = {}, {}
            for h0 in range(0, N_HEADS, piece):
                for h in range(h0, h0 + piece):
                    s_bf[h] = s_ref[h].astype(BF16)
                    vs = jnp.concatenate([v_ref[rows[ci], cols[h]], s_bf[h]], axis=0)
                    v_new[h] = _dot(uw[ci, h][0], vs).astype(BF16)
                yield
            for h0 in range(0, N_HEADS, piece):
                for h in range(h0, h0 + piece):
                    k_dec = (k_ref[rows[ci], cols[h]].astype(F32)
                             * lane_bcast(gt["e_rest"], g_lane[h])).astype(BF16)
                    e_end = jnp.broadcast_to(gt["e_end"][:, g_lane[h]:g_lane[h] + 1], (c, c))
                    s_ref[h] = s_ref[h] * e_end + lax.dot_general(
                        k_dec, v_new[h], tn_dims, preferred_element_type=F32)
                yield
            for h0 in range(0, N_HEADS, piece):
                for h in range(h0, h0 + piece):
                    o = _dot(uw[ci, h][1], jnp.concatenate([s_bf[h], v_new[h]], axis=0))
                    if add_first:
                        o = o + ofwd_ref[rows[ci], cols[h]]
                    o_ref[rows[ci], cols[h]] = o
                yield

    order = list(range(nch - 1, -1, -1) if reverse else range(nch))
    groups = [order[n:n + DELTA_GROUP_CHUNKS] for n in range(0, nch, DELTA_GROUP_CHUNKS)]
    stages = (prep_stage, inverse_stage, scan_stage)
    done = object()
    for tick in range(len(groups) + len(stages) - 1):
        running = [stage(groups[tick - d]) for d, stage in enumerate(stages)
                   if 0 <= tick - d < len(groups)]
        while running:
            running = [g for g in running if next(g, done) is not done]


def _delta(q, k, v, gates, reverse, ofwd=None):
    b, t, _ = q.shape
    c = DELTA_CHUNK * min(DELTA_STEP_CHUNKS, t // DELTA_CHUNK)
    nc = t // c
    add_first = ofwd is not None
    if reverse:
        rows = lambda bi, i: (bi, nc - 1 - i, 0)
    else:
        rows = lambda bi, i: (bi, i, 0)
    wide = pl.BlockSpec((None, c, QK_W), rows)
    in_specs = [wide, wide, wide, pl.BlockSpec((None, c, LANES), rows)]
    args = [q, k, v, gates]
    if add_first:
        in_specs += [wide]
        args += [ofwd]
    return pl.pallas_call(
        functools.partial(_delta_kernel, reverse=reverse, add_first=add_first),
        grid=(b, nc),
        in_specs=in_specs,
        out_specs=wide,
        out_shape=jax.ShapeDtypeStruct((b, t, QK_W), F32),
        scratch_shapes=[pltpu.VMEM((N_HEADS, HEAD_D, HEAD_D), F32)],
        compiler_params=pltpu.CompilerParams(dimension_semantics=("arbitrary", "arbitrary")),
        name="delta_bwd" if reverse else "delta_fwd",
    )(*args)


def _out_proj_kernel(o_ref_in, z_ref, onw_ref, f_ref, zf_ref, ga_ref, gb_ref, x_ref,
                     wpa_ref, wpb_ref, wo_ref, fnw_ref, o_ref, og_ref, *, final):
    ones = jnp.ones((HEAD_D, HEAD_D), BF16)
    onw = onw_ref[...]
    for h in range(N_HEADS):
        cols = slice(h * HEAD_D, (h + 1) * HEAD_D)
        oh = o_ref_in[:, cols]
        ms = _dot((oh * oh).astype(BF16), ones) * (1.0 / HEAD_D)
        gate = _silu(z_ref[:, cols].astype(F32))
        og_ref[:, cols] = (oh * lax.rsqrt(ms + EPS) * onw * gate).astype(BF16)
    ya = _dot(og_ref[...], wpa_ref[...])
    fg = (f_ref[...] * _silu(zf_ref[...].astype(F32))).astype(BF16)
    yb = _dot(fg, wpb_ref[...])
    merged = (_sigmoid(ga_ref[...].astype(F32)) * ya + _sigmoid(gb_ref[...].astype(F32)) * yb)
    xo = x_ref[...] + _dot(merged.astype(BF16), wo_ref[...])
    if final:
        ms = jnp.mean(xo * xo, axis=-1, keepdims=True)
        xo = xo * lax.rsqrt(ms + EPS) * fnw_ref[...]
    o_ref[...] = xo


def _out_proj(o, z, onw, f, zf, ga, gb, x, wpa, wpb, wo, fnw, final):
    b, t, d = x.shape
    tb = min(ROW_BLOCK, t)
    rows = lambda bi, i: (bi, i, 0)
    const = lambda bi, i: (0, 0)
    row_spec = lambda w: pl.BlockSpec((None, tb, w), rows)
    full = lambda a: pl.BlockSpec(a.shape, const)
    return pl.pallas_call(
        functools.partial(_out_proj_kernel, final=final),
        grid=(b, t // tb),
        in_specs=[row_spec(QK_W), row_spec(QK_W), full(onw), row_spec(F_W), row_spec(F_W),
                  row_spec(d), row_spec(d), row_spec(d),
                  full(wpa), full(wpb), full(wo), full(fnw)],
        out_specs=row_spec(d),
        out_shape=jax.ShapeDtypeStruct((b, t, d), F32),
        scratch_shapes=[pltpu.VMEM((tb, QK_W), BF16)],
        compiler_params=pltpu.CompilerParams(
            dimension_semantics=("parallel", "parallel"), vmem_limit_bytes=VMEM_LIMIT),
        name="out_proj",
    )(o, z, onw, f, zf, ga, gb, x, wpa, wpb, wo, fnw)


def _pad_lanes(v):
    flat = v.reshape(1, -1).astype(F32)
    return jnp.pad(flat, ((0, 0), (0, LANES - flat.shape[1])))


def _layer_params(norm_w, w_in, conv_w, a_log, dt_bias, o_norm_w, w_pa, w_pb, w_o):
    o_z = 3 * QK_W
    o_ab = o_z + QK_W
    o_uf = o_ab + 2 * N_DIR * N_HEADS
    wqkv = w_in[:, :o_z].astype(BF16)
    wrest = jnp.concatenate([w_in[:, o_z:o_ab], w_in[:, o_uf:]], axis=1).astype(BF16)
    wab = jnp.pad(w_in[:, o_ab:o_uf], ((0, 0), (0, LANES - 2 * N_DIR * N_HEADS)))
    wabh = wab.astype(BF16)
    wab = jnp.concatenate([wabh, (wab - wabh.astype(F32)).astype(BF16)], axis=1)
    convw = jnp.pad(conv_w, ((0, SUBLANES - CONV_K), (0, 0)))
    return dict(norm_w=norm_w.reshape(1, -1), wqkv=wqkv, wrest=wrest, wab=wab,
                convw=convw, alog=_pad_lanes(a_log), dtb=_pad_lanes(dt_bias),
                onw=o_norm_w.reshape(1, -1), wpa=w_pa.astype(BF16), wpb=w_pb.astype(BF16),
                wo=w_o.astype(BF16))


def _layer(x, p, fnw, final):
    q, k, v, z, gates, uf, zf, ga, gb = _in_proj(
        x, p["norm_w"], p["wqkv"], p["wrest"], p["wab"], p["convw"], p["alog"], p["dtb"])
    f = _fft_real(uf)
    o_fwd = _delta(q, k, v, gates, reverse=False)
    o = _delta(q, k, v, gates, reverse=True, ofwd=o_fwd)
    return _out_proj(o, z, p["onw"], f, zf, ga, gb, x, p["wpa"], p["wpb"], p["wo"], fnw, final)


def _trunk(x, params, fnw):
    depth = len(params)
    for l, p in enumerate(params):
        x = _layer(x, p, fnw, final=(l == depth - 1))
    return x


def kernel(x_prompt, x_sample, norm_w, w_in, conv_w, a_log, dt_bias, o_norm_w, w_pa, w_pb, w_o,
           final_norm_w):
    depth = norm_w.shape[0]
    params = [_layer_params(norm_w[l], w_in[l], conv_w[l], a_log[l], dt_bias[l], o_norm_w[l],
                            w_pa[l], w_pb[l], w_o[l]) for l in range(depth)]
    fnw = final_norm_w.reshape(1, -1)
    return (_trunk(x_prompt, params, fnw), _trunk(x_sample, params, fnw))
```
